```python
import math
import jax, jax.numpy as jnp
from jax import lax
import numpy as np

D_MODEL = 1024
BATCH = 16
SEQ = 2048
DEPTH = 2

HGRN_HEADS = 4
HGRN_DK = 128
HGRN_DV = 128
HGRN_WIDTH = HGRN_HEADS * HGRN_DK
HGRN_CHUNK = 64
ATT_HEADS = 8
ATT_DH = 64
ATT_WIDTH = ATT_HEADS * ATT_DH
IDX_HEADS = 8
IDX_DH = 64
TOPK_MAX = 256
Q_BLOCK = 128
REL_BUCKETS = 32
REL_MAX_DIST = 128
N_EXPERTS = 32
TOP_K = 4
D_FF = 1024
SWIGLU_ALPHA = 1.702
SWIGLU_LIMIT = 7.0
MOE_BLOCK = 256
NORM_EPS = 1e-6
N_MOD = 6

IN_WIDTHS = (
    HGRN_WIDTH, HGRN_WIDTH, HGRN_HEADS * HGRN_DV, HGRN_HEADS * HGRN_DV,
    ATT_WIDTH, ATT_DH, ATT_DH,
    IDX_HEADS * IDX_DH, IDX_DH, IDX_HEADS,
    D_MODEL, D_MODEL,
)
IN_COLS = sum(IN_WIDTHS)

kernel_name = "hybrid_hgrn2_dsa_moe_adaln"


def rms_norm(x, w):
    xf = x.astype(jnp.float32)
    y = xf * lax.rsqrt(jnp.mean(xf * xf, axis=-1, keepdims=True) + NORM_EPS)
    return (y * w.astype(jnp.float32)).astype(x.dtype)


def split_columns(z):
    parts, off = [], 0
    for w in IN_WIDTHS:
        parts.append(z[..., off:off + w])
        off += w
    return parts


def t5_bucket(rel):
    max_exact = REL_BUCKETS // 2
    rel_f = jnp.maximum(rel, 1).astype(jnp.float32)
    large = max_exact + (jnp.log(rel_f / max_exact) / math.log(REL_MAX_DIST / max_exact)
                         * (REL_BUCKETS - max_exact)).astype(jnp.int32)
    large = jnp.minimum(large, REL_BUCKETS - 1)
    return jnp.where(rel < max_exact, rel, large)


def hgrn2_chunk_scan(q, k, v, log_f):
    b_, seq_len, heads, dk = q.shape
    dv = v.shape[-1]
    n_chunks = seq_len // HGRN_CHUNK

    def to_chunks(t):
        return t.reshape(b_, n_chunks, HGRN_CHUNK, heads, t.shape[-1]).transpose(1, 0, 3, 2, 4)

    causal = jnp.tril(jnp.ones((HGRN_CHUNK, HGRN_CHUNK), bool))

    def step(state, inp):
        qc, kc, vc, gc = inp
        b = jnp.cumsum(gc, axis=2)
        diff = b[:, :, :, None, :] - b[:, :, None, :, :]
        decay = jnp.exp(jnp.where(causal[:, :, None], diff, -jnp.inf))
        a = jnp.einsum('bhtd,bhsd,bhtsd->bhts', qc, kc, decay)
        o = (jnp.einsum('bhts,bhsv->bhtv', a, vc)
             + jnp.einsum('bhtd,bhdv->bhtv', qc * jnp.exp(b), state))
        b_last = b[:, :, -1:, :]
        state = (jnp.exp(b_last[:, :, 0, :])[..., None] * state
                 + jnp.einsum('bhsd,bhsv->bhdv', kc * jnp.exp(b_last - b), vc))
        return state, o

    state0 = jnp.zeros((b_, heads, dk, dv), jnp.float32)
    _, o = lax.scan(step, state0, (to_chunks(q), to_chunks(k), to_chunks(v), to_chunks(log_f)))
    return o.transpose(1, 0, 3, 2, 4).reshape(b_, seq_len, heads, dv)


def hgrn2_branch(qa, fa, ia, ga, lb, norm_w):
    b_, seq_len, _ = qa.shape
    shp = (b_, seq_len, HGRN_HEADS, HGRN_DK)
    q = jax.nn.silu(qa.astype(jnp.float32)).reshape(shp)
    z = fa.astype(jnp.float32).reshape(shp)
    lbh = lb.reshape(HGRN_HEADS, HGRN_DK)
    log_f = jnp.logaddexp(jnp.log(lbh), jnp.log1p(-lbh) + jax.nn.log_sigmoid(z))
    k = (1.0 - lbh) * jax.nn.sigmoid(-z)
    v = ia.astype(jnp.float32).reshape(b_, seq_len, HGRN_HEADS, HGRN_DV)
    o = hgrn2_chunk_scan(q, k, v, log_f)
    o = o * lax.rsqrt(jnp.mean(o * o, axis=-1, keepdims=True) + NORM_EPS)
    o = o.reshape(b_, seq_len, HGRN_HEADS * HGRN_DV) * norm_w.astype(jnp.float32)
    return (o * jax.nn.silu(ga.astype(jnp.float32))).astype(qa.dtype)


def dsa_branch(qb, kb, vb, iq, ik, iw, rel_bias):
    b_, seq_len, _ = qb.shape
    top_k = min(TOPK_MAX, seq_len // 4)
    n_blocks = seq_len // Q_BLOCK
    q = qb.reshape(b_, seq_len, ATT_HEADS, ATT_DH)
    iq = iq.reshape(b_, seq_len, IDX_HEADS, IDX_DH).astype(jnp.float32)
    ik = ik.astype(jnp.float32)
    iw = iw.astype(jnp.float32) * (IDX_HEADS ** -0.5 * IDX_DH ** -0.5)
    key_pos = jnp.arange(seq_len)
    gather_rows = jax.vmap(lambda kv, ix: kv[ix])

    def block(i):
        start = i * Q_BLOCK
        t = start + jnp.arange(Q_BLOCK)
        q_b = lax.dynamic_slice_in_dim(q, start, Q_BLOCK, axis=1).astype(jnp.float32)
        iq_b = lax.dynamic_slice_in_dim(iq, start, Q_BLOCK, axis=1)
        iw_b = lax.dynamic_slice_in_dim(iw, start, Q_BLOCK, axis=1)
        s = jnp.einsum('bqhd,bsd->bqhs', iq_b, ik)
        score = jnp.einsum('bqh,bqhs->bqs', iw_b, jax.nn.relu(s))
        score = jnp.where(key_pos[None, None, :] <= t[None, :, None], score, -jnp.inf)
        _, idx = lax.top_k(score, top_k)
        valid = idx <= t[None, :, None]
        k_sel = gather_rows(kb, idx).astype(jnp.float32)
        v_sel = gather_rows(vb, idx).astype(jnp.float32)
        bias = rel_bias[t5_bucket(jnp.maximum(t[None, :, None] - idx, 0))]
        logits = (jnp.einsum('bqhd,bqkd->bqhk', q_b, k_sel) * ATT_DH ** -0.5
                  + bias.transpose(0, 1, 3, 2).astype(jnp.float32))
        logits = jnp.where(valid[:, :, None, :], logits, -jnp.inf)
        p = jax.nn.softmax(logits, axis=-1)
        return jnp.einsum('bqhk,bqkd->bqhd', p, v_sel).astype(qb.dtype)

    out = lax.map(block, jnp.arange(n_blocks))
    return out.transpose(1, 0, 2, 3, 4).reshape(b_, seq_len, ATT_WIDTH)


def mixer_block(h, w_in, lb, hgrn_norm_w, rel_bias, w_proj_a, w_proj_b, w_out):
    z = h @ w_in
    qa, fa, ia, ga, qb, kb, vb, iq, ik, iw, gate_a, gate_b = split_columns(z)
    y_a = hgrn2_branch(qa, fa, ia, ga, lb, hgrn_norm_w) @ w_proj_a
    y_b = dsa_branch(qb, kb, vb, iq, ik, iw, rel_bias) @ w_proj_b
    merged = jax.nn.sigmoid(gate_a) * y_a + jax.nn.sigmoid(gate_b) * y_b
    return merged @ w_out


def moe_block(h, w_router, b_router, w1, b1, w2, b2):
    n_tok, d = h.shape
    n_assign = n_tok * TOP_K
    logits = (h @ w_router + b_router).astype(jnp.float32)
    top_val, top_idx = lax.top_k(logits, TOP_K)
    gates = jax.nn.softmax(top_val, axis=-1).astype(h.dtype)
    flat_e = top_idx.reshape(-1)
    order = jnp.argsort(flat_e)
    sorted_e = flat_e[order]
    tok = order // TOP_K
    counts = jnp.bincount(flat_e, length=N_EXPERTS)
    padded = (counts + MOE_BLOCK - 1) // MOE_BLOCK * MOE_BLOCK
    pad_end = jnp.cumsum(padded)
    pad_start = pad_end - padded
    start = jnp.cumsum(counts) - counts
    dest = pad_start[sorted_e] + (jnp.arange(n_assign) - start[sorted_e])
    n_slots = -(-n_assign // MOE_BLOCK) * MOE_BLOCK + N_EXPERTS * MOE_BLOCK
    n_blk = n_slots // MOE_BLOCK
    slot_tok = jnp.full((n_slots,), n_tok, jnp.int32).at[dest].set(tok.astype(jnp.int32))
    h_pad = jnp.concatenate([h, jnp.zeros((1, d), h.dtype)], axis=0)
    xs = h_pad[slot_tok].reshape(n_blk, MOE_BLOCK, d)
    block_expert = jnp.minimum(
        jnp.searchsorted(pad_end, jnp.arange(n_blk) * MOE_BLOCK, side='right'), N_EXPERTS - 1)

    def expert_block(args):
        xb, e = args
        hm = xb @ w1[e] + b1[e]
        glu = jnp.minimum(hm[:, :D_FF], SWIGLU_LIMIT)
        lin = jnp.clip(hm[:, D_FF:], -SWIGLU_LIMIT, SWIGLU_LIMIT)
        act = glu * jax.nn.sigmoid(SWIGLU_ALPHA * glu) * (lin + 1.0)
        return act @ w2[e] + b2[e]

    y_slots = lax.map(expert_block, (xs, block_expert)).reshape(n_slots, d)
    dest_orig = jnp.zeros((n_assign,), jnp.int32).at[order].set(dest.astype(jnp.int32))
    y_sel = y_slots[dest_orig].reshape(n_tok, TOP_K, d)
    return jnp.einsum('tk,tkd->td', gates, y_sel)


def setup_inputs(seed: int = 0) -> dict:
    key = jax.random.key(seed)
    ks = jax.random.split(key, 20)

    def nrm(k, shape, fan_in, scale=1.0):
        return jax.random.normal(k, shape, jnp.float32) * (scale * fan_in ** -0.5)

    def gain(k, shape):
        return 1.0 + 0.02 * jax.random.normal(k, shape, jnp.float32)

    return {
        "x": jax.random.normal(ks[0], (BATCH, SEQ, D_MODEL), jnp.float32),
        "c": jax.random.normal(ks[1], (BATCH, D_MODEL), jnp.float32),
        "w_ada": nrm(ks[2], (DEPTH, D_MODEL, N_MOD * D_MODEL), D_MODEL, 0.5),
        "b_ada": 0.02 * jax.random.normal(ks[3], (DEPTH, N_MOD * D_MODEL), jnp.float32),
        "norm_mix_w": gain(ks[4], (DEPTH, D_MODEL)),
        "w_in": nrm(ks[5], (DEPTH, D_MODEL, IN_COLS), D_MODEL),
        "hgrn_lb_logits": 0.5 * jax.random.normal(ks[6], (DEPTH, HGRN_WIDTH), jnp.float32),
        "hgrn_norm_w": gain(ks[7], (DEPTH, HGRN_HEADS * HGRN_DV)),
        "rel_bias": 0.5 * jax.random.normal(ks[8], (REL_BUCKETS, ATT_HEADS), jnp.float32),
        "w_proj_a": nrm(ks[9], (DEPTH, HGRN_HEADS * HGRN_DV, D_MODEL), HGRN_HEADS * HGRN_DV),
        "w_proj_b": nrm(ks[10], (DEPTH, ATT_WIDTH, D_MODEL), ATT_WIDTH),
        "w_out": nrm(ks[11], (DEPTH, D_MODEL, D_MODEL), D_MODEL),
        "norm_ffn_w": gain(ks[12], (DEPTH, D_MODEL)),
        "w_router": nrm(ks[13], (DEPTH, D_MODEL, N_EXPERTS), D_MODEL),
        "b_router": 0.01 * jax.random.normal(ks[14], (DEPTH, N_EXPERTS), jnp.float32),
        "w_mlp1": nrm(ks[15], (DEPTH, N_EXPERTS, D_MODEL, 2 * D_FF), D_MODEL),
        "b_mlp1": 0.02 * jax.random.normal(ks[16], (DEPTH, N_EXPERTS, 2 * D_FF), jnp.float32),
        "w_mlp2": nrm(ks[17], (DEPTH, N_EXPERTS, D_FF, D_MODEL), D_FF),
        "b_mlp2": 0.02 * jax.random.normal(ks[18], (DEPTH, N_EXPERTS, D_MODEL), jnp.float32),
        "final_norm_w": gain(ks[19], (D_MODEL,)),
    }


def reference(x, c, w_ada, b_ada, norm_mix_w, w_in, hgrn_lb_logits, hgrn_norm_w, rel_bias,
              w_proj_a, w_proj_b, w_out, norm_ffn_w, w_router, b_router,
              w_mlp1, b_mlp1, w_mlp2, b_mlp2, final_norm_w):
    b_, seq_len, d = x.shape
    lb_all = jnp.cumsum(jax.nn.softmax(hgrn_lb_logits.astype(jnp.float32), axis=0), axis=0)
    lb_all = lb_all - lb_all[0]
    c_act = jax.nn.silu(c)
    for l in range(DEPTH):
        mod = c_act @ w_ada[l] + b_ada[l]
        shift_m, scale_m, gate_m, shift_f, scale_f, gate_f = jnp.split(mod, N_MOD, axis=-1)
        h = rms_norm(x, norm_mix_w[l]) * (1.0 + scale_m[:, None, :]) + shift_m[:, None, :]
        y = mixer_block(h, w_in[l], lb_all[l], hgrn_norm_w[l], rel_bias,
                        w_proj_a[l], w_proj_b[l], w_out[l])
        x = x + gate_m[:, None, :] * y
        h = rms_norm(x, norm_ffn_w[l]) * (1.0 + scale_f[:, None, :]) + shift_f[:, None, :]
        y = moe_block(h.reshape(b_ * seq_len, d), w_router[l], b_router[l],
                      w_mlp1[l], b_mlp1[l], w_mlp2[l], b_mlp2[l])
        x = x + gate_f[:, None, :] * y.reshape(b_, seq_len, d)
    return rms_norm(x, final_norm_w)
```

```python
import functools
import math

import jax
import jax.numpy as jnp
from jax import lax
from jax.experimental import pallas as pl
from jax.experimental.pallas import tpu as pltpu

F32 = jnp.float32
BF16 = jnp.bfloat16
I32 = jnp.int32

HGRN_HEADS = 4
HGRN_DK = 128
ATT_HEADS = 8
ATT_DH = 64
IDX_HEADS = 8
IDX_DH = 64
TOPK_MAX = 256
REL_BUCKETS = 32
REL_MAX_DIST = 128
TOP_K = 4
SWIGLU_ALPHA = 1.702
SWIGLU_LIMIT = 7.0
NORM_EPS = 1e-6
N_MOD = 6

LANES = 128
HGRN_CHUNK = 128
HGRN_SUB = 16
KEY_BLOCK = 128
MOE_BLOCK = 256
VMEM_LIMIT = 56 * 1024 * 1024
NEG_INF = float("-inf")
INT_MIN = -(2 ** 31)


def _dot(a, b):
    return jnp.dot(a, b, preferred_element_type=F32)


def _dot_nt(a, b):
    return lax.dot_general(a, b, (((1,), (1,)), ((), ())), preferred_element_type=F32)


def _dot_tn(a, b):
    return lax.dot_general(a, b, (((0,), (0,)), ((), ())), preferred_element_type=F32)


def _split2(a):
    hi = a.astype(BF16)
    lo = (a - hi.astype(F32)).astype(BF16)
    return hi, lo


def _split3(a):
    hi = a.astype(BF16)
    r = a - hi.astype(F32)
    mid = r.astype(BF16)
    lo = (r - mid.astype(F32)).astype(BF16)
    return hi, mid, lo


def _sigmoid(x):
    return 1.0 / (1.0 + jnp.exp(-x))


def _params(sem):
    return pltpu.CompilerParams(dimension_semantics=sem, vmem_limit_bytes=VMEM_LIMIT)


def _ada_body(c_ref, w_ref, b_ref, o_ref):
    c = c_ref[...]
    ca = c * _sigmoid(c)
    o_ref[0] = _dot(ca, w_ref[0]) + b_ref[0]


def _ada(c, w_ada, b_ada):
    depth, d, nd = w_ada.shape
    bsz = c.shape[0]
    tn = 1024
    return pl.pallas_call(
        _ada_body,
        grid=(depth, nd // tn),
        in_specs=[
            pl.BlockSpec((bsz, d), lambda l, j: (0, 0)),
            pl.BlockSpec((1, d, tn), lambda l, j: (l, 0, j)),
            pl.BlockSpec((1, 1, tn), lambda l, j: (l, 0, j)),
        ],
        out_specs=pl.BlockSpec((1, bsz, tn), lambda l, j: (l, 0, j)),
        out_shape=jax.ShapeDtypeStruct((depth, bsz, nd), F32),
        compiler_params=_params(("parallel", "parallel")),
        name="ada_mod",
    )(c, w_ada, b_ada.reshape(depth, 1, nd))


def _rms_mod(x, nw, scale, shift):
    ms = jnp.mean(x * x, axis=-1, keepdims=True)
    y = x * lax.rsqrt(ms + NORM_EPS) * nw
    return y * (1.0 + scale) + shift


def _inproj_body(x_ref, sh_ref, sc_ref, nw_ref, w1, w2, w3, w4, zh, zq, zs, zg):
    h = _rms_mod(x_ref[...], nw_ref[...], sc_ref[0], sh_ref[0]).astype(BF16)
    zh[...] = _dot(h, w1[...])
    zq[...] = _dot(h, w2[...]).astype(BF16)
    zs[...] = _dot(h, w3[...])
    zg[...] = _dot(h, w4[...])


def _inproj(x2d, shift, scale, nw, ws, seq_len):
    t, d = x2d.shape
    tm = 256
    per_b = seq_len // tm
    w1, w2, w3, w4 = ws
    row = lambda i: (i, 0)
    const = lambda i: (0, 0)
    bmap = lambda i: (i // per_b, 0, 0)
    outs = [
        jax.ShapeDtypeStruct((t, w1.shape[1]), F32),
        jax.ShapeDtypeStruct((t, w2.shape[1]), BF16),
        jax.ShapeDtypeStruct((t, w3.shape[1]), F32),
        jax.ShapeDtypeStruct((t, w4.shape[1]), F32),
    ]
    return pl.pallas_call(
        _inproj_body,
        grid=(t // tm,),
        in_specs=[
            pl.BlockSpec((tm, d), row),
            pl.BlockSpec((1, 1, d), bmap),
            pl.BlockSpec((1, 1, d), bmap),
            pl.BlockSpec((1, d), const),
            pl.BlockSpec(w1.shape, const),
            pl.BlockSpec(w2.shape, const),
            pl.BlockSpec(w3.shape, const),
            pl.BlockSpec(w4.shape, const),
        ],
        out_specs=[pl.BlockSpec((tm, o.shape[1]), row) for o in outs],
        out_shape=outs,
        compiler_params=_params(("parallel",)),
        name="in_proj",
    )(x2d, shift, scale, nw, w1, w2, w3, w4)


def _hgrn_body(lbl_ref, qa_ref, fa_ref, ia_ref, ga_ref, nw_ref, o_ref, *, layer, seq_len):
    C, R = HGRN_CHUNK, HGRN_SUB
    lg = lbl_ref[...]
    ex = jnp.exp(lg - jnp.max(lg, axis=0, keepdims=True))
    sm = ex / jnp.sum(ex, axis=0, keepdims=True)
    lb = jnp.zeros((1, LANES), F32)
    for li in range(1, layer + 1):
        lb = lb + sm[li]
    log_lb = jnp.log(lb)
    log_1m = jnp.log1p(-lb)
    nw = nw_ref[...]

    rowi = lax.broadcasted_iota(I32, (C, LANES), 0)
    ri2 = lax.broadcasted_iota(I32, (C, C), 0)
    ci2 = lax.broadcasted_iota(I32, (C, C), 1)
    ltri = jnp.where(ci2 <= ri2, 1.0, 0.0).astype(BF16)
    ones_bf = jnp.ones((LANES, LANES), BF16)
    dlt = ri2 - ci2

    def chunk(ci, st):
        r0 = pl.multiple_of(ci * C, C)
        qa = qa_ref[0, pl.ds(r0, C), :]
        z = fa_ref[0, pl.ds(r0, C), :]
        v = ia_ref[0, pl.ds(r0, C), :]
        ga = ga_ref[0, pl.ds(r0, C), :]
        q = qa * _sigmoid(qa)
        ls = jnp.minimum(z, 0.0) - jnp.log1p(jnp.exp(-jnp.abs(z)))
        t2 = log_1m + ls
        mx = jnp.maximum(log_lb, t2)
        g = mx + jnp.log(jnp.exp(log_lb - mx) + jnp.exp(t2 - mx))
        k = (1.0 - lb) * _sigmoid(-z)
        g1, g2, g3 = _split3(g)
        b = _dot(ltri, g1) + (_dot(ltri, g2) + _dot(ltri, g3))

        a = jnp.zeros((C, C), F32)
        half = C // 2
        while half >= R:
            span = 2 * half
            nb = C // span
            refb = jnp.concatenate(
                [jnp.broadcast_to(b[p * span + half - 1:p * span + half, :], (span, LANES)) for p in range(nb)],
                axis=0)
            second = (rowi & (span - 1)) >= half
            qt = q * jnp.exp(jnp.where(second, b - refb, NEG_INF))
            kt = k * jnp.exp(jnp.where(second, NEG_INF, refb - b))
            al = _dot_nt(qt.astype(BF16), kt.astype(BF16))
            sh = int(math.log2(span))
            a = a + jnp.where((ri2 >> sh) == (ci2 >> sh), al, 0.0)
            half //= 2

        ps = []
        for dl in range(R):
            if dl == 0:
                kd, bd = k, b
            else:
                kd = pltpu.roll(k, dl, 0)
                bd = pltpu.roll(b, dl, 0)
            arg = jnp.where((rowi & (R - 1)) >= dl, b - bd, NEG_INF)
            ps.append((q * kd * jnp.exp(arg)).astype(BF16))
        rs = _dot(jnp.concatenate(ps, axis=0), ones_bf)
        ad = jnp.zeros((C, C), F32)
        for dl in range(R):
            ad = jnp.where(dlt == dl, rs[dl * C:(dl + 1) * C, :], ad)
        shr = int(math.log2(R))
        a = a + jnp.where((ri2 >> shr) == (ci2 >> shr), ad, 0.0)

        vb = v.astype(BF16)
        o = _dot(a.astype(BF16), vb) + _dot_nt((q * jnp.exp(b)).astype(BF16), st.astype(BF16))
        bl = b[C - 1:C, :]
        kdec = (k * jnp.exp(bl - b)).astype(BF16)
        st_new = st * jnp.exp(bl) + _dot_tn(vb, kdec)
        on = o * lax.rsqrt(jnp.mean(o * o, axis=-1, keepdims=True) + NORM_EPS) * nw
        o_ref[0, pl.ds(r0, C), :] = (on * (ga * _sigmoid(ga))).astype(o_ref.dtype)
        return st_new

    lax.fori_loop(0, seq_len // C, chunk, jnp.zeros((LANES, LANES), F32))


def _hgrn(zh, lb_logits, norm_w, layer, bsz, seq_len):
    depth = lb_logits.shape[0]
    hw = HGRN_HEADS * HGRN_DK
    z3 = zh.reshape(bsz, seq_len, zh.shape[1])
    blk = lambda off: pl.BlockSpec((1, seq_len, LANES), lambda b, h: (b, 0, off + h))
    return pl.pallas_call(
        functools.partial(_hgrn_body, layer=layer, seq_len=seq_len),
        grid=(bsz, HGRN_HEADS),
        in_specs=[
            pl.BlockSpec((depth, 1, LANES), lambda b, h: (0, 0, h)),
            blk(0), blk(HGRN_HEADS), blk(2 * HGRN_HEADS), blk(3 * HGRN_HEADS),
            pl.BlockSpec((1, LANES), lambda b, h: (0, h)),
        ],
        out_specs=pl.BlockSpec((1, seq_len, LANES), lambda b, h: (b, 0, h)),
        out_shape=jax.ShapeDtypeStruct((bsz, seq_len, hw), BF16),
        compiler_params=_params(("parallel", "parallel")),
        name="hgrn2",
    )(lb_logits.reshape(depth, 1, hw), z3, z3, z3, z3, norm_w.reshape(1, hw))


def _float_key(x):
    x = jnp.where(x == 0.0, 0.0, x)
    bits = lax.bitcast_convert_type(x, I32)
    return bits ^ ((bits >> 31) & 0x7FFFFFFF)


def _dsa_body(zq_ref, zsq_ref, zs_ref, bias_ref, o_ref, key_ref, sel_ref, *, k_sel):
    i = pl.program_id(1)
    tq = KEY_BLOCK
    aw = ATT_HEADS * ATT_DH
    nb = i + 1
    rowi = lax.broadcasted_iota(I32, (tq, LANES), 0)
    coli = lax.broadcasted_iota(I32, (tq, LANES), 1)
    ones_bf = jnp.ones((LANES, LANES), BF16)
    sut = jnp.where(rowi < coli, 1.0, 0.0).astype(BF16)
    iw = zsq_ref[:, 3 * ATT_DH:3 * ATT_DH + IDX_HEADS] * (IDX_HEADS ** -0.5 * IDX_DH ** -0.5)
    iwb = [jnp.broadcast_to(iw[:, h:h + 1], (tq, LANES)) for h in range(IDX_HEADS)]

    def causal(j):
        return jnp.logical_or(j < i, coli <= rowi)

    def score_blk(j, c):
        r = pl.multiple_of(j * KEY_BLOCK, KEY_BLOCK)
        ikj = zs_ref[pl.ds(r, KEY_BLOCK), 2 * ATT_DH:3 * ATT_DH].astype(BF16)
        acc = jnp.zeros((tq, LANES), F32)
        for h in range(IDX_HEADS):
            s = _dot_nt(zq_ref[:, aw + h * IDX_DH:aw + (h + 1) * IDX_DH], ikj)
            acc = acc + iwb[h] * jnp.maximum(s, 0.0)
        key_ref[j] = _float_key(jnp.where(causal(j), acc, NEG_INF))
        return c

    lax.fori_loop(0, nb, score_blk, 0)

    kf = float(k_sel)

    def count(pred):
        def blk(j, acc):
            return acc + jnp.where(pred(key_ref[j]), 1.0, 0.0)
        acc = lax.fori_loop(0, nb, blk, jnp.zeros((tq, LANES), F32))
        return _dot(acc.astype(BF16), ones_bf)

    zero_i = jnp.zeros((tq, LANES), I32)
    thr0 = jnp.where(count(lambda kj: kj >= zero_i) >= kf, zero_i, INT_MIN)

    def bit_step(it, thr):
        cand = thr | jnp.left_shift(1, 30 - it)
        return jnp.where(count(lambda kj: kj >= cand) >= kf, cand, thr)

    thr = lax.fori_loop(0, 31, bit_step, thr0)
    need = kf - count(lambda kj: kj > thr)

    def sel_blk(j, carry):
        kj = key_ref[j]
        eq = jnp.where(kj == thr, 1.0, 0.0).astype(BF16)
        pref = _dot(eq, sut) + carry
        tie_ok = jnp.where(kj == thr, jnp.where(pref < need, 1.0, 0.0), 0.0)
        chosen = jnp.where(kj > thr, 1.0, tie_ok)
        chosen = jnp.where(causal(j), chosen, 0.0)
        sel_ref[j] = jnp.where(chosen > 0.0, 0.0, NEG_INF)
        return carry + _dot(eq, ones_bf)

    lax.fori_loop(0, nb, sel_blk, jnp.zeros((tq, LANES), F32))

    outs = []
    for h in range(ATT_HEADS):
        qh = zq_ref[:, h * ATT_DH:(h + 1) * ATT_DH] * (ATT_DH ** -0.5)

        def att_blk(j, carry, h=h, qh=qh):
            m, l, acc = carry
            r = pl.multiple_of(j * KEY_BLOCK, KEY_BLOCK)
            kj = zs_ref[pl.ds(r, KEY_BLOCK), 0:ATT_DH].astype(BF16)
            vj = zs_ref[pl.ds(r, KEY_BLOCK), ATT_DH:2 * ATT_DH].astype(BF16)
            s = _dot_nt(qh, kj) + bias_ref[jnp.minimum(i - j, 2), h] + sel_ref[j]
            m_new = jnp.maximum(m, jnp.max(s, axis=-1, keepdims=True))
            m_safe = jnp.where(m_new == NEG_INF, 0.0, m_new)
            p = jnp.exp(s - m_safe)
            alpha = jnp.exp(m - m_safe)
            l = alpha * l + jnp.sum(p, axis=-1, keepdims=True)
            acc = alpha * acc + _dot(p.astype(BF16), vj)
            return m_new, l, acc

        init = (jnp.full((tq, 1), NEG_INF, F32), jnp.zeros((tq, 1), F32), jnp.zeros((tq, ATT_DH), F32))
        _, l, acc = lax.fori_loop(0, nb, att_blk, init)
        outs.append(acc / l)
    o_ref[...] = jnp.concatenate(outs, axis=-1).astype(o_ref.dtype)


def _t5_bucket(rel):
    max_exact = REL_BUCKETS // 2
    rel_f = jnp.maximum(rel, 1).astype(F32)
    large = max_exact + (jnp.log(rel_f / max_exact) / math.log(REL_MAX_DIST / max_exact)
                         * (REL_BUCKETS - max_exact)).astype(I32)
    large = jnp.minimum(large, REL_BUCKETS - 1)
    return jnp.where(rel < max_exact, rel, large)


def _bias_tiles(rel_bias):
    kb = KEY_BLOCK
    assert kb >= REL_MAX_DIST
    tq = jnp.arange(kb)[:, None]
    sk = jnp.arange(kb)[None, :]
    tiles = []
    for off in range(3):
        rel = jnp.maximum(off * kb + tq - sk, 0)
        tiles.append(jnp.transpose(rel_bias[_t5_bucket(rel)], (2, 0, 1)))
    return jnp.stack(tiles, axis=0).astype(F32)


def _dsa(zq, zs, bias_tiles, bsz, seq_len):
    t = zq.shape[0]
    nq = seq_len // KEY_BLOCK
    k_sel = min(TOPK_MAX, seq_len // 4)
    aw = ATT_HEADS * ATT_DH
    return pl.pallas_call(
        functools.partial(_dsa_body, k_sel=k_sel),
        grid=(bsz, nq),
        in_specs=[
            pl.BlockSpec((KEY_BLOCK, zq.shape[1]), lambda b, i: (b * nq + i, 0)),
            pl.BlockSpec((KEY_BLOCK, zs.shape[1]), lambda b, i: (b * nq + i, 0)),
            pl.BlockSpec((seq_len, zs.shape[1]), lambda b, i: (b, 0)),
            pl.BlockSpec(bias_tiles.shape, lambda b, i: (0, 0, 0, 0)),
        ],
        out_specs=pl.BlockSpec((KEY_BLOCK, aw), lambda b, i: (b * nq + i, 0)),
        out_shape=jax.ShapeDtypeStruct((t, aw), BF16),
        scratch_shapes=[pltpu.VMEM((nq, KEY_BLOCK, LANES), I32), pltpu.VMEM((nq, KEY_BLOCK, LANES), F32)],
        compiler_params=_params(("parallel", "arbitrary")),
        name="dsa",
    )(zq, zs, zs, bias_tiles)


def _merge_body(x_ref, oa_ref, ob_ref, zg_ref, gm_ref, shf_ref, scf_ref, nw_ref,
                wpa, wpb, wout, wrt, br, x1_ref, h2_ref, lgt_ref):
    d = x_ref.shape[1]
    ya = _dot(oa_ref[...], wpa[...])
    yb = _dot(ob_ref[...], wpb[...])
    merged = _sigmoid(zg_ref[:, :d]) * ya + _sigmoid(zg_ref[:, d:]) * yb
    y = _dot(merged.astype(BF16), wout[...])
    x1 = x_ref[...] + gm_ref[0] * y
    x1_ref[...] = x1
    h2 = _rms_mod(x1, nw_ref[...], scf_ref[0], shf_ref[0])
    h2_ref[...] = h2
    hh, hl = _split2(h2)
    wh, wl = _split2(wrt[...])
    lgt_ref[...] = _dot_nt(wh, hh) + (_dot_nt(wh, hl) + _dot_nt(wl, hh)) + br[...]


def _merge(x2d, oa, ob, zg, gate_m, shift_f, scale_f, nw, wpa, wpb, wout, wrt, br, seq_len):
    t, d = x2d.shape
    tm = 256
    per_b = seq_len // tm
    e = wrt.shape[0]
    row = lambda i: (i, 0)
    const = lambda i: (0, 0)
    bmap = lambda i: (i // per_b, 0, 0)
    return pl.pallas_call(
        _merge_body,
        grid=(t // tm,),
        in_specs=[
            pl.BlockSpec((tm, d), row),
            pl.BlockSpec((tm, oa.shape[1]), row),
            pl.BlockSpec((tm, ob.shape[1]), row),
            pl.BlockSpec((tm, zg.shape[1]), row),
            pl.BlockSpec((1, 1, d), bmap),
            pl.BlockSpec((1, 1, d), bmap),
            pl.BlockSpec((1, 1, d), bmap),
            pl.BlockSpec((1, d), const),
            pl.BlockSpec(wpa.shape, const),
            pl.BlockSpec(wpb.shape, const),
            pl.BlockSpec(wout.shape, const),
            pl.BlockSpec(wrt.shape, const),
            pl.BlockSpec(br.shape, const),
        ],
        out_specs=[pl.BlockSpec((tm, d), row), pl.BlockSpec((tm, d), row),
                   pl.BlockSpec((e, tm), lambda i: (0, i))],
        out_shape=[jax.ShapeDtypeStruct((t, d), F32), jax.ShapeDtypeStruct((t, d), F32),
                   jax.ShapeDtypeStruct((e, t), F32)],
        compiler_params=_params(("parallel",)),
        name="merge",
    )(x2d, oa, ob, zg, gate_m, shift_f, scale_f, nw, wpa, wpb, wout, wrt, br)


def _route1_body(lg_ref, er_ref, gt_ref, cnt_ref, base_ref):
    i = pl.program_id(0)
    e, tr = lg_ref.shape

    @pl.when(i == 0)
    def _():
        base_ref[...] = jnp.zeros_like(base_ref)

    l = lg_ref[...]
    eidx = lax.broadcasted_iota(I32, (e, tr), 0).astype(F32)
    vals, es, ohs = [], [], []
    for _ in range(TOP_K):
        m = jnp.max(l, axis=0, keepdims=True)
        ek = jnp.min(jnp.where(l == m, eidx, float(e)), axis=0, keepdims=True)
        oh = eidx == ek
        l = jnp.where(oh, NEG_INF, l)
        vals.append(m)
        es.append(ek)
        ohs.append(oh)
    ps = [jnp.exp(v - vals[0]) for v in vals]
    den = ps[0] + ps[1] + ps[2] + ps[3]
    gs = [p / den for p in ps]

    member = jnp.zeros((e, tr), F32)
    for oh in ohs:
        member = member + jnp.where(oh, 1.0, 0.0)
    mb = member.astype(BF16)
    r2 = lax.broadcasted_iota(I32, (tr, tr), 0)
    c2 = lax.broadcasted_iota(I32, (tr, tr), 1)
    sut = jnp.where(r2 < c2, 1.0, 0.0).astype(BF16)
    base = base_ref[...]
    pref = _dot(mb, sut) + jnp.concatenate([base] * (tr // LANES), axis=1)
    ranks = [jnp.sum(jnp.where(oh, pref, 0.0), axis=0, keepdims=True) for oh in ohs]
    er_ref[...] = jnp.concatenate(es + ranks, axis=0).astype(I32)

    row8 = lax.broadcasted_iota(I32, (8, tr), 0)
    g8 = jnp.zeros((8, tr), F32)
    for kk in range(TOP_K):
        g8 = jnp.where(row8 == kk, gs[kk], g8)
    gfull = jnp.concatenate([g8, jnp.zeros((LANES - 8, tr), F32)], axis=0)
    gt_ref[...] = gfull.T

    new_base = base + _dot(mb, jnp.ones((tr, LANES), BF16))
    base_ref[...] = new_base
    cnt_ref[...] = new_base


def _route1(lgt):
    e, t = lgt.shape
    tr = 512
    return pl.pallas_call(
        _route1_body,
        grid=(t // tr,),
        in_specs=[pl.BlockSpec((e, tr), lambda i: (0, i))],
        out_specs=[pl.BlockSpec((2 * TOP_K, tr), lambda i: (0, i)),
                   pl.BlockSpec((tr, LANES), lambda i: (i, 0)),
                   pl.BlockSpec((e, LANES), lambda i: (0, 0))],
        out_shape=[jax.ShapeDtypeStruct((2 * TOP_K, t), I32),
                   jax.ShapeDtypeStruct((t, LANES), F32),
                   jax.ShapeDtypeStruct((e, LANES), F32)],
        scratch_shapes=[pltpu.VMEM((e, LANES), F32)],
        compiler_params=_params(("arbitrary",)),
        name="route_topk",
    )(lgt)


def _route2_body(er_ref, cnt_ref, slot_ref, be_ref, tot_ref):
    e = cnt_ref.shape[0]
    tt = er_ref.shape[1]
    nbp = be_ref.shape[1]
    sh = int(math.log2(MOE_BLOCK))
    cnt = cnt_ref[...].astype(I32)
    padded = ((cnt + (MOE_BLOCK - 1)) >> sh) << sh
    rows = []
    acc = jnp.zeros((1, LANES), I32)
    for ei in range(e):
        rows.append(acc)
        acc = acc + padded[ei:ei + 1, :]
    pstart = jnp.concatenate(rows, axis=0)
    pend = pstart + padded
    tot_ref[...] = jnp.broadcast_to(acc, tot_ref.shape)

    er = er_ref[...]
    eidx = lax.broadcasted_iota(I32, (e, tt), 0)
    pst = jnp.concatenate([pstart] * (tt // LANES), axis=1)
    outs = []
    for kk in range(TOP_K):
        ps = jnp.sum(jnp.where(eidx == er[kk:kk + 1, :], pst, 0), axis=0, keepdims=True)
        outs.append(ps + er[TOP_K + kk:TOP_K + kk + 1, :])
    slot_ref[...] = jnp.concatenate(outs + [jnp.zeros((TOP_K, tt), I32)], axis=0)

    blk0 = lax.broadcasted_iota(I32, (e, nbp), 1) * MOE_BLOCK
    pe = jnp.concatenate([pend] * (nbp // LANES), axis=1)
    nle = jnp.sum(jnp.where(pe <= blk0, 1, 0), axis=0, keepdims=True)
    be_ref[...] = jnp.broadcast_to(jnp.minimum(nle, e - 1), be_ref.shape)


def _route2(er, cnt, nbp):
    t = er.shape[1]
    tt = min(2048, t)
    e = cnt.shape[0]
    return pl.pallas_call(
        _route2_body,
        grid=(t // tt,),
        in_specs=[pl.BlockSpec((2 * TOP_K, tt), lambda i: (0, i)),
                  pl.BlockSpec((e, LANES), lambda i: (0, 0))],
        out_specs=[pl.BlockSpec((2 * TOP_K, tt), lambda i: (0, i)),
                   pl.BlockSpec((8, nbp), lambda i: (0, 0)),
                   pl.BlockSpec((8, LANES), lambda i: (0, 0))],
        out_shape=[jax.ShapeDtypeStruct((2 * TOP_K, t), I32),
                   jax.ShapeDtypeStruct((8, nbp), I32),
                   jax.ShapeDtypeStruct((8, LANES), I32)],
        compiler_params=_params(("arbitrary",)),
        name="route_slots",
    )(er, cnt)


def _dispatch_body(slot_ref, h_ref, xs_in, xs_ref, sem):
    del xs_in
    i = pl.program_id(0)
    td = slot_ref.shape[1]

    def issue(tok, c):
        for kk in range(TOP_K):
            pltpu.make_async_copy(h_ref.at[pl.ds(i * td + tok, 1)],
                                  xs_ref.at[pl.ds(slot_ref[kk, tok], 1)], sem).start()
        return c

    lax.fori_loop(0, td, issue, 0)
    for kk in range(TOP_K):
        pltpu.make_async_copy(h_ref.at[pl.ds(0, td)], xs_ref.at[pl.ds(0, td)], sem).wait()


def _dispatch(slot, h2, n_slots):
    t, d = h2.shape
    td = 512
    xs0 = jnp.zeros((n_slots, d), h2.dtype)
    return pl.pallas_call(
        _dispatch_body,
        grid=(t // td,),
        in_specs=[pl.BlockSpec((2 * TOP_K, td), lambda i: (0, i), memory_space=pltpu.SMEM),
                  pl.BlockSpec(memory_space=pl.ANY),
                  pl.BlockSpec(memory_space=pl.ANY)],
        out_specs=pl.BlockSpec(memory_space=pl.ANY),
        out_shape=jax.ShapeDtypeStruct((n_slots, d), h2.dtype),
        scratch_shapes=[pltpu.SemaphoreType.DMA],
        input_output_aliases={2: 0},
        compiler_params=_params(("arbitrary",)),
        name="moe_dispatch",
    )(slot, h2, xs0)


def _expert_body(be_ref, nu_ref, xs_ref, w1_ref, b1_ref, w2_ref, b2_ref, y_ref):
    del be_ref
    i = pl.program_id(0)
    f = w2_ref.shape[1]

    @pl.when(i < nu_ref[0])
    def _():
        x = xs_ref[...].astype(BF16)
        hm = _dot(x, w1_ref[0]) + b1_ref[0]
        glu = jnp.minimum(hm[:, :f], SWIGLU_LIMIT)
        lin = jnp.clip(hm[:, f:], -SWIGLU_LIMIT, SWIGLU_LIMIT)
        act = glu * _sigmoid(SWIGLU_ALPHA * glu) * (lin + 1.0)
        y_ref[...] = _dot(act.astype(BF16), w2_ref[0]) + b2_ref[0]

    @pl.when(i >= nu_ref[0])
    def _():
        y_ref[...] = jnp.zeros_like(y_ref)


def _experts(be, nused, xs, w1, b1, w2, b2):
    ns, d = xs.shape
    e, _, f2 = w1.shape
    f = w2.shape[1]
    nblk = ns // MOE_BLOCK
    grid_spec = pltpu.PrefetchScalarGridSpec(
        num_scalar_prefetch=2,
        grid=(nblk,),
        in_specs=[
            pl.BlockSpec((MOE_BLOCK, d), lambda i, be, nu: (i, 0)),
            pl.BlockSpec((1, d, f2), lambda i, be, nu: (be[i], 0, 0)),
            pl.BlockSpec((1, 1, f2), lambda i, be, nu: (be[i], 0, 0)),
            pl.BlockSpec((1, f, d), lambda i, be, nu: (be[i], 0, 0)),
            pl.BlockSpec((1, 1, d), lambda i, be, nu: (be[i], 0, 0)),
        ],
        out_specs=pl.BlockSpec((MOE_BLOCK, d), lambda i, be, nu: (i, 0)),
    )
    return pl.pallas_call(
        _expert_body,
        grid_spec=grid_spec,
        out_shape=jax.ShapeDtypeStruct((ns, d), F32),
        compiler_params=_params(("arbitrary",)),
        name="moe_experts",
    )(be, nused, xs, w1, b1.reshape(e, 1, f2), w2, b2.reshape(e, 1, d))


def _combine_body(slot_ref, y_ref, x1_ref, gt_ref, gf_ref, fw_ref, o_ref, buf, sem, *, final):
    td = x1_ref.shape[0]

    def issue(tok, c):
        for kk in range(TOP_K):
            pltpu.make_async_copy(y_ref.at[pl.ds(slot_ref[kk, tok], 1)],
                                  buf.at[kk, pl.ds(tok, 1)], sem).start()
        return c

    lax.fori_loop(0, td, issue, 0)
    for kk in range(TOP_K):
        pltpu.make_async_copy(y_ref.at[pl.ds(0, td)], buf.at[kk], sem).wait()
    g = gt_ref[...]
    y = g[:, 0:1] * buf[0]
    for kk in range(1, TOP_K):
        y = y + g[:, kk:kk + 1] * buf[kk]
    x2 = x1_ref[...] + gf_ref[0] * y
    if final:
        ms = jnp.mean(x2 * x2, axis=-1, keepdims=True)
        x2 = x2 * lax.rsqrt(ms + NORM_EPS) * fw_ref[...]
    o_ref[...] = x2


def _combine(slot, ys, x1, gt, gate_f, fw, seq_len, final):
    t, d = x1.shape
    td = 256
    per_b = seq_len // td
    return pl.pallas_call(
        functools.partial(_combine_body, final=final),
        grid=(t // td,),
        in_specs=[pl.BlockSpec((2 * TOP_K, td), lambda i: (0, i), memory_space=pltpu.SMEM),
                  pl.BlockSpec(memory_space=pl.ANY),
                  pl.BlockSpec((td, d), lambda i: (i, 0)),
                  pl.BlockSpec((td, LANES), lambda i: (i, 0)),
                  pl.BlockSpec((1, 1, d), lambda i: (i // per_b, 0, 0)),
                  pl.BlockSpec((1, d), lambda i: (0, 0))],
        out_specs=pl.BlockSpec((td, d), lambda i: (i, 0)),
        out_shape=jax.ShapeDtypeStruct((t, d), F32),
        scratch_shapes=[pltpu.VMEM((TOP_K, td, d), F32), pltpu.SemaphoreType.DMA],
        compiler_params=_params(("arbitrary",)),
        name="moe_combine",
    )(slot, ys, x1, gt, gate_f, fw)


def _split_w_in(w):
    hw = HGRN_HEADS * HGRN_DK
    aw = ATT_HEADS * ATT_DH
    iw = IDX_HEADS * IDX_DH
    d = w.shape[0]
    o = 4 * hw
    qb = w[:, o:o + aw]
    kb = w[:, o + aw:o + aw + ATT_DH]
    vb = w[:, o + aw + ATT_DH:o + aw + 2 * ATT_DH]
    o2 = o + aw + 2 * ATT_DH
    iq = w[:, o2:o2 + iw]
    ik = w[:, o2 + iw:o2 + iw + IDX_DH]
    ih = w[:, o2 + iw + IDX_DH:o2 + iw + IDX_DH + IDX_HEADS]
    o3 = o2 + iw + IDX_DH + IDX_HEADS
    small = jnp.concatenate([kb, vb, ik, ih, jnp.zeros((d, 2 * LANES - 3 * ATT_DH - IDX_HEADS), w.dtype)], axis=1)
    groups = (w[:, :o], jnp.concatenate([qb, iq], axis=1), small, w[:, o3:])
    return tuple(g.astype(BF16) for g in groups)


def kernel(x, c, w_ada, b_ada, norm_mix_w, w_in, hgrn_lb_logits, hgrn_norm_w, rel_bias, w_proj_a, w_proj_b,
           w_out, norm_ffn_w, w_router, b_router, w_mlp1, b_mlp1, w_mlp2, b_mlp2, final_norm_w):
    bsz, seq_len, d = x.shape
    depth = w_in.shape[0]
    n_exp = w_router.shape[2]
    t = bsz * seq_len
    n_slots = t * TOP_K + n_exp * MOE_BLOCK
    nblk = n_slots // MOE_BLOCK
    nbp = -(-nblk // LANES) * LANES

    mod = _ada(c, w_ada, b_ada)
    bias_tiles = _bias_tiles(rel_bias)
    x2d = x.reshape(t, d)
    for l in range(depth):
        m6 = mod[l].reshape(bsz, N_MOD, 1, d)
        shift_m, scale_m, gate_m, shift_f, scale_f, gate_f = (m6[:, n] for n in range(N_MOD))
        zh, zq, zs, zg = _inproj(x2d, shift_m, scale_m, norm_mix_w[l].reshape(1, d), _split_w_in(w_in[l]), seq_len)
        oa = _hgrn(zh, hgrn_lb_logits, hgrn_norm_w[l], l, bsz, seq_len).reshape(t, -1)
        ob = _dsa(zq, zs, bias_tiles, bsz, seq_len)
        x1, h2, lgt = _merge(x2d, oa, ob, zg, gate_m, shift_f, scale_f, norm_ffn_w[l].reshape(1, d),
                             w_proj_a[l].astype(BF16), w_proj_b[l].astype(BF16), w_out[l].astype(BF16),
                             w_router[l].T, b_router[l].reshape(n_exp, 1), seq_len)
        er, gt, cnt = _route1(lgt)
        slot, be, tot = _route2(er, cnt, nbp)
        xs = _dispatch(slot, h2, n_slots)
        nused = (tot[0, :1] >> int(math.log2(MOE_BLOCK))).astype(I32)
        ys = _experts(be[0, :nblk], nused, xs, w_mlp1[l].astype(BF16), b_mlp1[l], w_mlp2[l].astype(BF16), b_mlp2[l])
        x2d = _combine(slot, ys, x1, gt, gate_f, final_norm_w.reshape(1, d), seq_len, final=(l == depth - 1))
    return x2d.reshape(bsz, seq_len, d)
```

```python
import functools
import math

import jax
import jax.numpy as jnp
from jax import lax
from jax.experimental import pallas as pl
from jax.experimental.pallas import tpu as pltpu

F32 = jnp.float32
BF16 = jnp.bfloat16
I32 = jnp.int32

HGRN_HEADS = 4
HGRN_DK = 128
ATT_HEADS = 8
ATT_DH = 64
IDX_HEADS = 8
IDX_DH = 64
TOPK_MAX = 256
REL_BUCKETS = 32
REL_MAX_DIST = 128
TOP_K = 4
SWIGLU_ALPHA = 1.702
SWIGLU_LIMIT = 7.0
NORM_EPS = 1e-6
N_MOD = 6

LANES = 128
HGRN_CHUNK = 128
HGRN_SUB = 16
KEY_BLOCK = 128
Q_ROWS = 256
MOE_BLOCK = 256
VMEM_LIMIT = 56 * 1024 * 1024
NEG_INF = float("-inf")
INT_MIN = -(2 ** 31)


def _dot(a, b):
    return jnp.dot(a, b, preferred_element_type=F32)


def _dot_nt(a, b):
    return lax.dot_general(a, b, (((1,), (1,)), ((), ())), preferred_element_type=F32)


def _dot_tn(a, b):
    return lax.dot_general(a, b, (((0,), (0,)), ((), ())), preferred_element_type=F32)


def _split2(a):
    hi = a.astype(BF16)
    lo = (a - hi.astype(F32)).astype(BF16)
    return hi, lo


def _split3(a):
    hi = a.astype(BF16)
    r = a - hi.astype(F32)
    mid = r.astype(BF16)
    lo = (r - mid.astype(F32)).astype(BF16)
    return hi, mid, lo


def _sigmoid(x):
    return 1.0 / (1.0 + jnp.exp(-x))


def _params(sem):
    return pltpu.CompilerParams(dimension_semantics=sem, vmem_limit_bytes=VMEM_LIMIT)


def _ada_body(c_ref, w_ref, b_ref, o_ref):
    c = c_ref[...]
    ca = c * _sigmoid(c)
    o_ref[0] = _dot(ca, w_ref[0]) + b_ref[0]


def _ada(c, w_ada, b_ada):
    depth, d, nd = w_ada.shape
    bsz = c.shape[0]
    tn = 1024
    return pl.pallas_call(
        _ada_body,
        grid=(depth, nd // tn),
        in_specs=[
            pl.BlockSpec((bsz, d), lambda l, j: (0, 0)),
            pl.BlockSpec((1, d, tn), lambda l, j: (l, 0, j)),
            pl.BlockSpec((1, 1, tn), lambda l, j: (l, 0, j)),
        ],
        out_specs=pl.BlockSpec((1, bsz, tn), lambda l, j: (l, 0, j)),
        out_shape=jax.ShapeDtypeStruct((depth, bsz, nd), F32),
        compiler_params=_params(("parallel", "parallel")),
        name="ada_mod",
    )(c, w_ada, b_ada.reshape(depth, 1, nd))


def _rms_mod(x, nw, scale, shift):
    ms = jnp.mean(x * x, axis=-1, keepdims=True)
    y = x * lax.rsqrt(ms + NORM_EPS) * nw
    return y * (1.0 + scale) + shift


def _inproj_body(x_ref, sh_ref, sc_ref, nw_ref, w1, w2, w3, w4, zh, zq, zs, zg):
    h = _rms_mod(x_ref[...], nw_ref[...], sc_ref[0], sh_ref[0]).astype(BF16)
    zh[...] = _dot(h, w1[...])
    zq[...] = _dot(h, w2[...]).astype(BF16)
    zs[...] = _dot(h, w3[...])
    zg[...] = _dot(h, w4[...])


def _inproj(x2d, shift, scale, nw, ws, seq_len):
    t, d = x2d.shape
    tm = 256
    per_b = seq_len // tm
    w1, w2, w3, w4 = ws
    row = lambda i: (i, 0)
    const = lambda i: (0, 0)
    bmap = lambda i: (i // per_b, 0, 0)
    outs = [
        jax.ShapeDtypeStruct((t, w1.shape[1]), F32),
        jax.ShapeDtypeStruct((t, w2.shape[1]), BF16),
        jax.ShapeDtypeStruct((t, w3.shape[1]), F32),
        jax.ShapeDtypeStruct((t, w4.shape[1]), F32),
    ]
    return pl.pallas_call(
        _inproj_body,
        grid=(t // tm,),
        in_specs=[
            pl.BlockSpec((tm, d), row),
            pl.BlockSpec((1, 1, d), bmap),
            pl.BlockSpec((1, 1, d), bmap),
            pl.BlockSpec((1, d), const),
            pl.BlockSpec(w1.shape, const),
            pl.BlockSpec(w2.shape, const),
            pl.BlockSpec(w3.shape, const),
            pl.BlockSpec(w4.shape, const),
        ],
        out_specs=[pl.BlockSpec((tm, o.shape[1]), row) for o in outs],
        out_shape=outs,
        compiler_params=_params(("parallel",)),
        name="in_proj",
    )(x2d, shift, scale, nw, w1, w2, w3, w4)


def _hgrn_body(lbl_ref, qa_ref, fa_ref, ia_ref, ga_ref, nw_ref, o_ref, *, layer, seq_len):
    C, R = HGRN_CHUNK, HGRN_SUB
    lg = lbl_ref[...]
    ex = jnp.exp(lg - jnp.max(lg, axis=0, keepdims=True))
    sm = ex / jnp.sum(ex, axis=0, keepdims=True)
    lb = jnp.zeros((1, LANES), F32)
    for li in range(1, layer + 1):
        lb = lb + sm[li]
    log_lb = jnp.log(lb)
    log_1m = jnp.log1p(-lb)
    nw = nw_ref[...]

    rowi = lax.broadcasted_iota(I32, (C, LANES), 0)
    ri2 = lax.broadcasted_iota(I32, (C, C), 0)
    ci2 = lax.broadcasted_iota(I32, (C, C), 1)
    ltri = jnp.where(ci2 <= ri2, 1.0, 0.0).astype(BF16)
    ones_bf = jnp.ones((LANES, LANES), BF16)
    dlt = ri2 - ci2

    def chunk(ci, st):
        r0 = pl.multiple_of(ci * C, C)
        qa = qa_ref[0, pl.ds(r0, C), :]
        z = fa_ref[0, pl.ds(r0, C), :]
        v = ia_ref[0, pl.ds(r0, C), :]
        ga = ga_ref[0, pl.ds(r0, C), :]
        q = qa * _sigmoid(qa)
        ls = jnp.minimum(z, 0.0) - jnp.log1p(jnp.exp(-jnp.abs(z)))
        t2 = log_1m + ls
        mx = jnp.maximum(log_lb, t2)
        g = mx + jnp.log(jnp.exp(log_lb - mx) + jnp.exp(t2 - mx))
        k = (1.0 - lb) * _sigmoid(-z)
        g1, g2, g3 = _split3(g)
        b = _dot(ltri, g1) + (_dot(ltri, g2) + _dot(ltri, g3))

        a = jnp.zeros((C, C), F32)
        half = C // 2
        while half >= R:
            span = 2 * half
            nb = C // span
            refb = jnp.concatenate(
                [jnp.broadcast_to(b[p * span + half - 1:p * span + half, :], (span, LANES)) for p in range(nb)],
                axis=0)
            second = (rowi & (span - 1)) >= half
            qt = q * jnp.exp(jnp.where(second, b - refb, NEG_INF))
            kt = k * jnp.exp(jnp.where(second, NEG_INF, refb - b))
            al = _dot_nt(qt.astype(BF16), kt.astype(BF16))
            sh = int(math.log2(span))
            a = a + jnp.where((ri2 >> sh) == (ci2 >> sh), al, 0.0)
            half //= 2

        ps = []
        for dl in range(R):
            if dl == 0:
                kd, bd = k, b
            else:
                kd = pltpu.roll(k, dl, 0)
                bd = pltpu.roll(b, dl, 0)
            arg = jnp.where((rowi & (R - 1)) >= dl, b - bd, NEG_INF)
            ps.append((q * kd * jnp.exp(arg)).astype(BF16))
        rs = _dot(jnp.concatenate(ps, axis=0), ones_bf)
        ad = jnp.zeros((C, C), F32)
        for dl in range(R):
            ad = jnp.where(dlt == dl, rs[dl * C:(dl + 1) * C, :], ad)
        shr = int(math.log2(R))
        a = a + jnp.where((ri2 >> shr) == (ci2 >> shr), ad, 0.0)

        vb = v.astype(BF16)
        o = _dot(a.astype(BF16), vb) + _dot_nt((q * jnp.exp(b)).astype(BF16), st.astype(BF16))
        bl = b[C - 1:C, :]
        kdec = (k * jnp.exp(bl - b)).astype(BF16)
        st_new = st * jnp.exp(bl) + _dot_tn(vb, kdec)
        on = o * lax.rsqrt(jnp.mean(o * o, axis=-1, keepdims=True) + NORM_EPS) * nw
        o_ref[0, pl.ds(r0, C), :] = (on * (ga * _sigmoid(ga))).astype(o_ref.dtype)
        return st_new

    lax.fori_loop(0, seq_len // C, chunk, jnp.zeros((LANES, LANES), F32))


def _hgrn(zh, lb_logits, norm_w, layer, bsz, seq_len):
    depth = lb_logits.shape[0]
    hw = HGRN_HEADS * HGRN_DK
    z3 = zh.reshape(bsz, seq_len, zh.shape[1])
    blk = lambda off: pl.BlockSpec((1, seq_len, LANES), lambda b, h: (b, 0, off + h))
    return pl.pallas_call(
        functools.partial(_hgrn_body, layer=layer, seq_len=seq_len),
        grid=(bsz, HGRN_HEADS),
        in_specs=[
            pl.BlockSpec((depth, 1, LANES), lambda b, h: (0, 0, h)),
            blk(0), blk(HGRN_HEADS), blk(2 * HGRN_HEADS), blk(3 * HGRN_HEADS),
            pl.BlockSpec((1, LANES), lambda b, h: (0, h)),
        ],
        out_specs=pl.BlockSpec((1, seq_len, LANES), lambda b, h: (b, 0, h)),
        out_shape=jax.ShapeDtypeStruct((bsz, seq_len, hw), BF16),
        compiler_params=_params(("parallel", "parallel")),
        name="hgrn2",
    )(lb_logits.reshape(depth, 1, hw), z3, z3, z3, z3, norm_w.reshape(1, hw))


def _float_key(x):
    x = jnp.where(x == 0.0, 0.0, x)
    bits = lax.bitcast_convert_type(x, I32)
    return bits ^ ((bits >> 31) & 0x7FFFFFFF)


def _dsa_body(zq_ref, zsq_ref, zs_ref, bias_ref, o_ref,
              key_ref, sel_ref, k2_ref, ik2_ref, v2_ref, mx_ref, acc_ref, *, k_sel):
    i = pl.program_id(1)
    tq, kb = Q_ROWS, KEY_BLOCK
    sub = tq // kb
    nkb = key_ref.shape[0]
    aw = ATT_HEADS * ATT_DH
    npair = ATT_HEADS // 2
    half = LANES // 2
    nb = sub * (i + 1)

    @pl.when(i == 0)
    def _():
        lo = lax.broadcasted_iota(I32, (kb, LANES), 1) < half
        one = jnp.where(lo, 1.0, 0.0)

        def build(j, c):
            r = pl.multiple_of(j * kb, kb)
            kv = zs_ref[pl.ds(r, kb), 0:LANES]
            ix = zs_ref[pl.ds(r, kb), LANES:2 * LANES]
            vk = pltpu.roll(kv, half, 1)
            xi = pltpu.roll(ix, half, 1)
            sc = ATT_DH ** -0.5
            k2_ref[j] = jnp.concatenate([jnp.where(lo, kv * sc, 0.0), jnp.where(lo, 0.0, vk * sc)], axis=0).astype(BF16)
            ik2_ref[j] = jnp.concatenate([jnp.where(lo, ix, 0.0), jnp.where(lo, 0.0, xi)], axis=0).astype(BF16)
            top = jnp.concatenate([jnp.where(lo, vk, 0.0), one], axis=1)
            bot = jnp.concatenate([jnp.where(lo, 0.0, kv), 1.0 - one], axis=1)
            v2_ref[j] = jnp.concatenate([top, bot], axis=0).astype(BF16)
            return c

        lax.fori_loop(0, nkb, build, 0)

    rowi = lax.broadcasted_iota(I32, (tq, LANES), 0)
    coli = lax.broadcasted_iota(I32, (tq, LANES), 1)
    r1 = lax.broadcasted_iota(I32, (LANES, LANES), 0)
    c1 = lax.broadcasted_iota(I32, (LANES, LANES), 1)
    ones_bf = jnp.ones((LANES, LANES), BF16)
    sut = jnp.where(r1 < c1, 1.0, 0.0).astype(BF16)
    iw = zsq_ref[:, 3 * ATT_DH:3 * ATT_DH + IDX_HEADS] * (IDX_HEADS ** -0.5 * IDX_DH ** -0.5)
    iwb = [jnp.broadcast_to(iw[:, h:h + 1], (tq, LANES)) for h in range(IDX_HEADS)]

    def causal(j):
        return (kb * j + coli) <= (tq * i + rowi)

    def score_blk(j, c):
        acc = jnp.zeros((tq, LANES), F32)
        for p in range(IDX_HEADS // 2):
            s = _dot_nt(zq_ref[:, aw + p * LANES:aw + (p + 1) * LANES], ik2_ref[j])
            acc = acc + (iwb[2 * p] * jnp.maximum(s[:, :LANES], 0.0) + iwb[2 * p + 1] * jnp.maximum(s[:, LANES:], 0.0))
        key_ref[j] = _float_key(jnp.where(causal(j), acc, NEG_INF))
        return c

    lax.fori_loop(0, nb, score_blk, 0)

    kf = float(k_sel)

    def count(pred):
        def blk(j, acc):
            return acc + jnp.where(pred(key_ref[j]), 1.0, 0.0)
        acc = lax.fori_loop(0, nb, blk, jnp.zeros((tq, LANES), F32))
        return _dot(acc.astype(BF16), ones_bf)

    zero_i = jnp.zeros((tq, LANES), I32)
    thr0 = jnp.where(count(lambda kj: kj >= zero_i) >= kf, zero_i, INT_MIN)

    def bit_step(it, thr):
        cand = thr | jnp.left_shift(1, 30 - it)
        return jnp.where(count(lambda kj: kj >= cand) >= kf, cand, thr)

    thr = lax.fori_loop(0, 31, bit_step, thr0)
    need = kf - count(lambda kj: kj > thr)

    def sel_blk(j, carry):
        kj = key_ref[j]
        eq = jnp.where(kj == thr, 1.0, 0.0).astype(BF16)
        pref = _dot(eq, sut) + carry
        tie_ok = jnp.where(kj == thr, jnp.where(pref < need, 1.0, 0.0), 0.0)
        chosen = jnp.where(kj > thr, 1.0, tie_ok)
        chosen = jnp.where(causal(j), chosen, 0.0)
        sel_ref[j] = jnp.where(chosen > 0.0, 0.0, NEG_INF)
        return carry + _dot(eq, ones_bf)

    lax.fori_loop(0, nb, sel_blk, jnp.zeros((tq, LANES), F32))

    def logits(j, p):
        bias = jnp.concatenate([bias_ref[jnp.clip(sub * i + u - j, 0, 2), p] for u in range(sub)], axis=0)
        sel = sel_ref[j]
        s = _dot_nt(zq_ref[:, p * LANES:(p + 1) * LANES], k2_ref[j])
        return s + bias + jnp.concatenate([sel, sel], axis=1)

    mx_ref[...] = jnp.full(mx_ref.shape, NEG_INF, F32)

    def max_blk(j, c):
        for p in range(npair):
            mx_ref[p] = jnp.maximum(mx_ref[p], logits(j, p))
        return c

    lax.fori_loop(0, nb, max_blk, 0)
    for p in range(npair):
        m = mx_ref[p]
        m0 = jnp.max(m[:, :LANES], axis=-1, keepdims=True)
        m1 = jnp.max(m[:, LANES:], axis=-1, keepdims=True)
        mx_ref[p] = jnp.concatenate([jnp.broadcast_to(m0, (tq, LANES)), jnp.broadcast_to(m1, (tq, LANES))], axis=1)

    acc_ref[...] = jnp.zeros(acc_ref.shape, F32)

    def pv_blk(j, c):
        for p in range(npair):
            pe = jnp.exp(logits(j, p) - mx_ref[p]).astype(BF16)
            acc_ref[p] = acc_ref[p] + _dot(pe, v2_ref[j])
        return c

    lax.fori_loop(0, nb, pv_blk, 0)
    for p in range(npair):
        a = acc_ref[p]
        o_ref[:, p * LANES:(p + 1) * LANES] = (a[:, :LANES] / a[:, LANES:]).astype(o_ref.dtype)


def _t5_bucket(rel):
    max_exact = REL_BUCKETS // 2
    rel_f = jnp.maximum(rel, 1).astype(F32)
    large = max_exact + (jnp.log(rel_f / max_exact) / math.log(REL_MAX_DIST / max_exact)
                         * (REL_BUCKETS - max_exact)).astype(I32)
    large = jnp.minimum(large, REL_BUCKETS - 1)
    return jnp.where(rel < max_exact, rel, large)


def _bias_tiles(rel_bias):
    kb = KEY_BLOCK
    assert kb >= REL_MAX_DIST
    tq = jnp.arange(kb)[:, None]
    sk = jnp.arange(kb)[None, :]
    tiles = []
    for off in range(3):
        rel = jnp.maximum(off * kb + tq - sk, 0)
        per_head = jnp.transpose(rel_bias[_t5_bucket(rel)], (2, 0, 1))
        tiles.append(jnp.concatenate([per_head[0::2], per_head[1::2]], axis=-1))
    return jnp.stack(tiles, axis=0).astype(F32)


def _dsa(zq, zs, bias_tiles, bsz, seq_len):
    t = zq.shape[0]
    nq = seq_len // Q_ROWS
    nkb = seq_len // KEY_BLOCK
    k_sel = min(TOPK_MAX, seq_len // 4)
    aw = ATT_HEADS * ATT_DH
    npair = ATT_HEADS // 2
    return pl.pallas_call(
        functools.partial(_dsa_body, k_sel=k_sel),
        grid=(bsz, nq),
        in_specs=[
            pl.BlockSpec((Q_ROWS, zq.shape[1]), lambda b, i: (b * nq + i, 0)),
            pl.BlockSpec((Q_ROWS, zs.shape[1]), lambda b, i: (b * nq + i, 0)),
            pl.BlockSpec((seq_len, zs.shape[1]), lambda b, i: (b, 0)),
            pl.BlockSpec(bias_tiles.shape, lambda b, i: (0, 0, 0, 0)),
        ],
        out_specs=pl.BlockSpec((Q_ROWS, aw), lambda b, i: (b * nq + i, 0)),
        out_shape=jax.ShapeDtypeStruct((t, aw), BF16),
        scratch_shapes=[
            pltpu.VMEM((nkb, Q_ROWS, LANES), I32),
            pltpu.VMEM((nkb, Q_ROWS, LANES), F32),
            pltpu.VMEM((nkb, 2 * KEY_BLOCK, LANES), BF16),
            pltpu.VMEM((nkb, 2 * KEY_BLOCK, LANES), BF16),
            pltpu.VMEM((nkb, 2 * KEY_BLOCK, 2 * LANES), BF16),
            pltpu.VMEM((npair, Q_ROWS, 2 * LANES), F32),
            pltpu.VMEM((npair, Q_ROWS, 2 * LANES), F32),
        ],
        compiler_params=_params(("arbitrary", "arbitrary")),
        name="dsa",
    )(zq, zs, zs, bias_tiles)


def _merge_body(x_ref, oa_ref, ob_ref, zg_ref, gm_ref, shf_ref, scf_ref, nw_ref,
                wpa, wpb, wout, wrt, br, x1_ref, h2_ref, lgt_ref):
    d = x_ref.shape[1]
    ya = _dot(oa_ref[...], wpa[...])
    yb = _dot(ob_ref[...], wpb[...])
    merged = _sigmoid(zg_ref[:, :d]) * ya + _sigmoid(zg_ref[:, d:]) * yb
    y = _dot(merged.astype(BF16), wout[...])
    x1 = x_ref[...] + gm_ref[0] * y
    x1_ref[...] = x1
    h2 = _rms_mod(x1, nw_ref[...], scf_ref[0], shf_ref[0])
    h2_ref[...] = h2
    hh, hl = _split2(h2)
    wh, wl = _split2(wrt[...])
    lgt_ref[...] = _dot_nt(wh, hh) + (_dot_nt(wh, hl) + _dot_nt(wl, hh)) + br[...]


def _merge(x2d, oa, ob, zg, gate_m, shift_f, scale_f, nw, wpa, wpb, wout, wrt, br, seq_len):
    t, d = x2d.shape
    tm = 256
    per_b = seq_len // tm
    e = wrt.shape[0]
    row = lambda i: (i, 0)
    const = lambda i: (0, 0)
    bmap = lambda i: (i // per_b, 0, 0)
    return pl.pallas_call(
        _merge_body,
        grid=(t // tm,),
        in_specs=[
            pl.BlockSpec((tm, d), row),
            pl.BlockSpec((tm, oa.shape[1]), row),
            pl.BlockSpec((tm, ob.shape[1]), row),
            pl.BlockSpec((tm, zg.shape[1]), row),
            pl.BlockSpec((1, 1, d), bmap),
            pl.BlockSpec((1, 1, d), bmap),
            pl.BlockSpec((1, 1, d), bmap),
            pl.BlockSpec((1, d), const),
            pl.BlockSpec(wpa.shape, const),
            pl.BlockSpec(wpb.shape, const),
            pl.BlockSpec(wout.shape, const),
            pl.BlockSpec(wrt.shape, const),
            pl.BlockSpec(br.shape, const),
        ],
        out_specs=[pl.BlockSpec((tm, d), row), pl.BlockSpec((tm, d), row),
                   pl.BlockSpec((e, tm), lambda i: (0, i))],
        out_shape=[jax.ShapeDtypeStruct((t, d), F32), jax.ShapeDtypeStruct((t, d), F32),
                   jax.ShapeDtypeStruct((e, t), F32)],
        compiler_params=_params(("parallel",)),
        name="merge",
    )(x2d, oa, ob, zg, gate_m, shift_f, scale_f, nw, wpa, wpb, wout, wrt, br)


def _route1_body(lg_ref, er_ref, gt_ref, cnt_ref, base_ref):
    i = pl.program_id(0)
    e, tr = lg_ref.shape

    @pl.when(i == 0)
    def _():
        base_ref[...] = jnp.zeros_like(base_ref)

    l = lg_ref[...]
    eidx = lax.broadcasted_iota(I32, (e, tr), 0).astype(F32)
    vals, es, ohs = [], [], []
    for _ in range(TOP_K):
        m = jnp.max(l, axis=0, keepdims=True)
        ek = jnp.min(jnp.where(l == m, eidx, float(e)), axis=0, keepdims=True)
        oh = eidx == ek
        l = jnp.where(oh, NEG_INF, l)
        vals.append(m)
        es.append(ek)
        ohs.append(oh)
    ps = [jnp.exp(v - vals[0]) for v in vals]
    den = ps[0] + ps[1] + ps[2] + ps[3]
    gs = [p / den for p in ps]

    member = jnp.zeros((e, tr), F32)
    for oh in ohs:
        member = member + jnp.where(oh, 1.0, 0.0)
    mb = member.astype(BF16)
    r2 = lax.broadcasted_iota(I32, (tr, tr), 0)
    c2 = lax.broadcasted_iota(I32, (tr, tr), 1)
    sut = jnp.where(r2 < c2, 1.0, 0.0).astype(BF16)
    base = base_ref[...]
    pref = _dot(mb, sut) + jnp.concatenate([base] * (tr // LANES), axis=1)
    ranks = [jnp.sum(jnp.where(oh, pref, 0.0), axis=0, keepdims=True) for oh in ohs]
    er_ref[...] = jnp.concatenate(es + ranks, axis=0).astype(I32)

    row8 = lax.broadcasted_iota(I32, (8, tr), 0)
    g8 = jnp.zeros((8, tr), F32)
    for kk in range(TOP_K):
        g8 = jnp.where(row8 == kk, gs[kk], g8)
    gfull = jnp.concatenate([g8, jnp.zeros((LANES - 8, tr), F32)], axis=0)
    gt_ref[...] = gfull.T

    new_base = base + _dot(mb, jnp.ones((tr, LANES), BF16))
    base_ref[...] = new_base
    cnt_ref[...] = new_base


def _route1(lgt):
    e, t = lgt.shape
    tr = 512
    return pl.pallas_call(
        _route1_body,
        grid=(t // tr,),
        in_specs=[pl.BlockSpec((e, tr), lambda i: (0, i))],
        out_specs=[pl.BlockSpec((2 * TOP_K, tr), lambda i: (0, i)),
                   pl.BlockSpec((tr, LANES), lambda i: (i, 0)),
                   pl.BlockSpec((e, LANES), lambda i: (0, 0))],
        out_shape=[jax.ShapeDtypeStruct((2 * TOP_K, t), I32),
                   jax.ShapeDtypeStruct((t, LANES), F32),
                   jax.ShapeDtypeStruct((e, LANES), F32)],
        scratch_shapes=[pltpu.VMEM((e, LANES), F32)],
        compiler_params=_params(("arbitrary",)),
        name="route_topk",
    )(lgt)


def _route2_body(er_ref, cnt_ref, slot_ref, be_ref, tot_ref):
    e = cnt_ref.shape[0]
    tt = er_ref.shape[1]
    nbp = be_ref.shape[1]
    sh = int(math.log2(MOE_BLOCK))
    cnt = cnt_ref[...].astype(I32)
    padded = ((cnt + (MOE_BLOCK - 1)) >> sh) << sh
    rows = []
    acc = jnp.zeros((1, LANES), I32)
    for ei in range(e):
        rows.append(acc)
        acc = acc + padded[ei:ei + 1, :]
    pstart = jnp.concatenate(rows, axis=0)
    pend = pstart + padded
    tot_ref[...] = jnp.broadcast_to(acc, tot_ref.shape)

    er = er_ref[...]
    eidx = lax.broadcasted_iota(I32, (e, tt), 0)
    pst = jnp.concatenate([pstart] * (tt // LANES), axis=1)
    outs = []
    for kk in range(TOP_K):
        ps = jnp.sum(jnp.where(eidx == er[kk:kk + 1, :], pst, 0), axis=0, keepdims=True)
        outs.append(ps + er[TOP_K + kk:TOP_K + kk + 1, :])
    slot_ref[...] = jnp.concatenate(outs + [jnp.zeros((TOP_K, tt), I32)], axis=0)

    blk0 = lax.broadcasted_iota(I32, (e, nbp), 1) * MOE_BLOCK
    pe = jnp.concatenate([pend] * (nbp // LANES), axis=1)
    nle = jnp.sum(jnp.where(pe <= blk0, 1, 0), axis=0, keepdims=True)
    be_ref[...] = jnp.broadcast_to(jnp.minimum(nle, e - 1), be_ref.shape)


def _route2(er, cnt, nbp):
    t = er.shape[1]
    tt = min(2048, t)
    e = cnt.shape[0]
    return pl.pallas_call(
        _route2_body,
        grid=(t // tt,),
        in_specs=[pl.BlockSpec((2 * TOP_K, tt), lambda i: (0, i)),
                  pl.BlockSpec((e, LANES), lambda i: (0, 0))],
        out_specs=[pl.BlockSpec((2 * TOP_K, tt), lambda i: (0, i)),
                   pl.BlockSpec((8, nbp), lambda i: (0, 0)),
                   pl.BlockSpec((8, LANES), lambda i: (0, 0))],
        out_shape=[jax.ShapeDtypeStruct((2 * TOP_K, t), I32),
                   jax.ShapeDtypeStruct((8, nbp), I32),
                   jax.ShapeDtypeStruct((8, LANES), I32)],
        compiler_params=_params(("arbitrary",)),
        name="route_slots",
    )(er, cnt)


def _dispatch_body(slot_ref, h_ref, xs_in, xs_ref, sem):
    del xs_in
    td = slot_ref.shape[1]

    def issue(tok, c):
        for kk in range(TOP_K):
            pltpu.make_async_copy(h_ref.at[pl.ds(tok, 1)],
                                  xs_ref.at[pl.ds(slot_ref[kk, tok], 1)], sem).start()
        return c

    lax.fori_loop(0, td, issue, 0)
    for kk in range(TOP_K):
        pltpu.make_async_copy(h_ref, xs_ref.at[pl.ds(0, td)], sem).wait()


def _dispatch(slot, h2, n_slots):
    t, d = h2.shape
    td = 512
    xs0 = jnp.zeros((n_slots, d), h2.dtype)
    return pl.pallas_call(
        _dispatch_body,
        grid=(t // td,),
        in_specs=[pl.BlockSpec((2 * TOP_K, td), lambda i: (0, i), memory_space=pltpu.SMEM),
                  pl.BlockSpec((td, d), lambda i: (i, 0)),
                  pl.BlockSpec(memory_space=pl.ANY)],
        out_specs=pl.BlockSpec(memory_space=pl.ANY),
        out_shape=jax.ShapeDtypeStruct((n_slots, d), h2.dtype),
        scratch_shapes=[pltpu.SemaphoreType.DMA],
        input_output_aliases={2: 0},
        compiler_params=_params(("arbitrary",)),
        name="moe_dispatch",
    )(slot, h2, xs0)


def _expert_body(be_ref, nu_ref, xs_ref, w1_ref, b1_ref, w2_ref, b2_ref, y_ref):
    del be_ref
    i = pl.program_id(0)
    f = w2_ref.shape[1]

    @pl.when(i < nu_ref[0])
    def _():
        x = xs_ref[...].astype(BF16)
        hm = _dot(x, w1_ref[0]) + b1_ref[0]
        glu = jnp.minimum(hm[:, :f], SWIGLU_LIMIT)
        lin = jnp.clip(hm[:, f:], -SWIGLU_LIMIT, SWIGLU_LIMIT)
        act = glu * _sigmoid(SWIGLU_ALPHA * glu) * (lin + 1.0)
        y_ref[...] = _dot(act.astype(BF16), w2_ref[0]) + b2_ref[0]

    @pl.when(i >= nu_ref[0])
    def _():
        y_ref[...] = jnp.zeros_like(y_ref)


def _experts(be, nused, xs, w1, b1, w2, b2):
    ns, d = xs.shape
    e, _, f2 = w1.shape
    f = w2.shape[1]
    nblk = ns // MOE_BLOCK
    grid_spec = pltpu.PrefetchScalarGridSpec(
        num_scalar_prefetch=2,
        grid=(nblk,),
        in_specs=[
            pl.BlockSpec((MOE_BLOCK, d), lambda i, be, nu: (i, 0)),
            pl.BlockSpec((1, d, f2), lambda i, be, nu: (be[i], 0, 0)),
            pl.BlockSpec((1, 1, f2), lambda i, be, nu: (be[i], 0, 0)),
            pl.BlockSpec((1, f, d), lambda i, be, nu: (be[i], 0, 0)),
            pl.BlockSpec((1, 1, d), lambda i, be, nu: (be[i], 0, 0)),
        ],
        out_specs=pl.BlockSpec((MOE_BLOCK, d), lambda i, be, nu: (i, 0)),
    )
    return pl.pallas_call(
        _expert_body,
        grid_spec=grid_spec,
        out_shape=jax.ShapeDtypeStruct((ns, d), F32),
        compiler_params=_params(("arbitrary",)),
        name="moe_experts",
    )(be, nused, xs, w1, b1.reshape(e, 1, f2), w2, b2.reshape(e, 1, d))


def _combine_body(slot_ref, y_ref, x1_ref, gt_ref, gf_ref, fw_ref, o_ref, buf, sem, *, final):
    td = x1_ref.shape[0]

    def issue(tok, c):
        for kk in range(TOP_K):
            pltpu.make_async_copy(y_ref.at[pl.ds(slot_ref[kk, tok], 1)],
                                  buf.at[kk, pl.ds(tok, 1)], sem).start()
        return c

    lax.fori_loop(0, td, issue, 0)
    for kk in range(TOP_K):
        pltpu.make_async_copy(y_ref.at[pl.ds(0, td)], buf.at[kk], sem).wait()
    g = gt_ref[...]
    y = g[:, 0:1] * buf[0]
    for kk in range(1, TOP_K):
        y = y + g[:, kk:kk + 1] * buf[kk]
    x2 = x1_ref[...] + gf_ref[0] * y
    if final:
        ms = jnp.mean(x2 * x2, axis=-1, keepdims=True)
        x2 = x2 * lax.rsqrt(ms + NORM_EPS) * fw_ref[...]
    o_ref[...] = x2


def _combine(slot, ys, x1, gt, gate_f, fw, seq_len, final):
    t, d = x1.shape
    td = 256
    per_b = seq_len // td
    return pl.pallas_call(
        functools.partial(_combine_body, final=final),
        grid=(t // td,),
        in_specs=[pl.BlockSpec((2 * TOP_K, td), lambda i: (0, i), memory_space=pltpu.SMEM),
                  pl.BlockSpec(memory_space=pl.ANY),
                  pl.BlockSpec((td, d), lambda i: (i, 0)),
                  pl.BlockSpec((td, LANES), lambda i: (i, 0)),
                  pl.BlockSpec((1, 1, d), lambda i: (i // per_b, 0, 0)),
                  pl.BlockSpec((1, d), lambda i: (0, 0))],
        out_specs=pl.BlockSpec((td, d), lambda i: (i, 0)),
        out_shape=jax.ShapeDtypeStruct((t, d), F32),
        scratch_shapes=[pltpu.VMEM((TOP_K, td, d), F32), pltpu.SemaphoreType.DMA],
        compiler_params=_params(("arbitrary",)),
        name="moe_combine",
    )(slot, ys, x1, gt, gate_f, fw)


def _split_w_in(w):
    hw = HGRN_HEADS * HGRN_DK
    aw = ATT_HEADS * ATT_DH
    iw = IDX_HEADS * IDX_DH
    d = w.shape[0]
    o = 4 * hw
    qb = w[:, o:o + aw]
    kb = w[:, o + aw:o + aw + ATT_DH]
    vb = w[:, o + aw + ATT_DH:o + aw + 2 * ATT_DH]
    o2 = o + aw + 2 * ATT_DH
    iq = w[:, o2:o2 + iw]
    ik = w[:, o2 + iw:o2 + iw + IDX_DH]
    ih = w[:, o2 + iw + IDX_DH:o2 + iw + IDX_DH + IDX_HEADS]
    o3 = o2 + iw + IDX_DH + IDX_HEADS
    small = jnp.concatenate([kb, vb, ik, ih, jnp.zeros((d, 2 * LANES - 3 * ATT_DH - IDX_HEADS), w.dtype)], axis=1)
    groups = (w[:, :o], jnp.concatenate([qb, iq], axis=1), small, w[:, o3:])
    return tuple(g.astype(BF16) for g in groups)


def kernel(x, c, w_ada, b_ada, norm_mix_w, w_in, hgrn_lb_logits, hgrn_norm_w, rel_bias, w_proj_a, w_proj_b,
           w_out, norm_ffn_w, w_router, b_router, w_mlp1, b_mlp1, w_mlp2, b_mlp2, final_norm_w):
    bsz, seq_len, d = x.shape
    depth = w_in.shape[0]
    n_exp = w_router.shape[2]
    t = bsz * seq_len
    n_slots = t * TOP_K + n_exp * MOE_BLOCK
    nblk = n_slots // MOE_BLOCK
    nbp = -(-nblk // LANES) * LANES

    mod = _ada(c, w_ada, b_ada)
    bias_tiles = _bias_tiles(rel_bias)
    x2d = x.reshape(t, d)
    for l in range(depth):
        m6 = mod[l].reshape(bsz, N_MOD, 1, d)
        shift_m, scale_m, gate_m, shift_f, scale_f, gate_f = (m6[:, n] for n in range(N_MOD))
        zh, zq, zs, zg = _inproj(x2d, shift_m, scale_m, norm_mix_w[l].reshape(1, d), _split_w_in(w_in[l]), seq_len)
        oa = _hgrn(zh, hgrn_lb_logits, hgrn_norm_w[l], l, bsz, seq_len).reshape(t, -1)
        ob = _dsa(zq, zs, bias_tiles, bsz, seq_len)
        x1, h2, lgt = _merge(x2d, oa, ob, zg, gate_m, shift_f, scale_f, norm_ffn_w[l].reshape(1, d),
                             w_proj_a[l].astype(BF16), w_proj_b[l].astype(BF16), w_out[l].astype(BF16),
                             w_router[l].T, b_router[l].reshape(n_exp, 1), seq_len)
        er, gt, cnt = _route1(lgt)
        slot, be, tot = _route2(er, cnt, nbp)
        xs = _dispatch(slot, h2, n_slots)
        nused = (tot[0, :1] >> int(math.log2(MOE_BLOCK))).astype(I32)
        ys = _experts(be[0, :nblk], nused, xs, w_mlp1[l].astype(BF16), b_mlp1[l], w_mlp2[l].astype(BF16), b_mlp2[l])
        x2d = _combine(slot, ys, x1, gt, gate_f, final_norm_w.reshape(1, d), seq_len, final=(l == depth - 1))
    return x2d.reshape(bsz, seq_len, d)
```

```python
import functools
import math

import jax
import jax.numpy as jnp
from jax import lax
from jax.experimental import pallas as pl
from jax.experimental.pallas import tpu as pltpu

F32 = jnp.float32
BF16 = jnp.bfloat16
I32 = jnp.int32

HGRN_HEADS = 4
HGRN_DK = 128
ATT_HEADS = 8
ATT_DH = 64
IDX_HEADS = 8
IDX_DH = 64
TOPK_MAX = 256
REL_BUCKETS = 32
REL_MAX_DIST = 128
TOP_K = 4
SWIGLU_ALPHA = 1.702
SWIGLU_LIMIT = 7.0
NORM_EPS = 1e-6
N_MOD = 6

LANES = 128
HGRN_CHUNK = 128
HGRN_SUB = 8
KEY_BLOCK = 128
Q_ROWS = 256
MOE_BLOCK = 256
VMEM_LIMIT = 56 * 1024 * 1024
NEG_INF = float("-inf")
INT_MIN = -(2 ** 31)


def _dot(a, b):
    return jnp.dot(a, b, preferred_element_type=F32)


def _dot_nt(a, b):
    return lax.dot_general(a, b, (((1,), (1,)), ((), ())), preferred_element_type=F32)


def _dot_tn(a, b):
    return lax.dot_general(a, b, (((0,), (0,)), ((), ())), preferred_element_type=F32)


def _split2(a):
    hi = a.astype(BF16)
    lo = (a - hi.astype(F32)).astype(BF16)
    return hi, lo


def _split3(a):
    hi = a.astype(BF16)
    r = a - hi.astype(F32)
    mid = r.astype(BF16)
    lo = (r - mid.astype(F32)).astype(BF16)
    return hi, mid, lo


def _sigmoid(x):
    return 0.5 * jnp.tanh(0.5 * x) + 0.5


def _params(sem):
    return pltpu.CompilerParams(dimension_semantics=sem, vmem_limit_bytes=VMEM_LIMIT)


def _ada_body(c_ref, w_ref, b_ref, o_ref):
    c = c_ref[...]
    ca = c * _sigmoid(c)
    o_ref[0] = _dot(ca, w_ref[0]) + b_ref[0]


def _ada(c, w_ada, b_ada):
    depth, d, nd = w_ada.shape
    bsz = c.shape[0]
    tn = 1024
    return pl.pallas_call(
        _ada_body,
        grid=(depth, nd // tn),
        in_specs=[
            pl.BlockSpec((bsz, d), lambda l, j: (0, 0)),
            pl.BlockSpec((1, d, tn), lambda l, j: (l, 0, j)),
            pl.BlockSpec((1, 1, tn), lambda l, j: (l, 0, j)),
        ],
        out_specs=pl.BlockSpec((1, bsz, tn), lambda l, j: (l, 0, j)),
        out_shape=jax.ShapeDtypeStruct((depth, bsz, nd), F32),
        compiler_params=_params(("parallel", "parallel")),
        name="ada_mod",
    )(c, w_ada, b_ada.reshape(depth, 1, nd))


def _rms_mod(x, nw, scale, shift):
    ms = jnp.mean(x * x, axis=-1, keepdims=True)
    y = x * lax.rsqrt(ms + NORM_EPS) * nw
    return y * (1.0 + scale) + shift


def _inproj_body(x_ref, sh_ref, sc_ref, nw_ref, w1, w2, w3, w4, zh, zq, zs, zg):
    h = _rms_mod(x_ref[...], nw_ref[...], sc_ref[0], sh_ref[0]).astype(BF16)
    zh[...] = _dot(h, w1[...])
    zq[...] = _dot(h, w2[...]).astype(BF16)
    zs[...] = _dot(h, w3[...])
    zg[...] = _dot(h, w4[...])


def _inproj(x2d, shift, scale, nw, ws, seq_len):
    t, d = x2d.shape
    tm = 256
    per_b = seq_len // tm
    w1, w2, w3, w4 = ws
    row = lambda i: (i, 0)
    const = lambda i: (0, 0)
    bmap = lambda i: (i // per_b, 0, 0)
    outs = [
        jax.ShapeDtypeStruct((t, w1.shape[1]), F32),
        jax.ShapeDtypeStruct((t, w2.shape[1]), BF16),
        jax.ShapeDtypeStruct((t, w3.shape[1]), F32),
        jax.ShapeDtypeStruct((t, w4.shape[1]), F32),
    ]
    return pl.pallas_call(
        _inproj_body,
        grid=(t // tm,),
        in_specs=[
            pl.BlockSpec((tm, d), row),
            pl.BlockSpec((1, 1, d), bmap),
            pl.BlockSpec((1, 1, d), bmap),
            pl.BlockSpec((1, d), const),
            pl.BlockSpec(w1.shape, const),
            pl.BlockSpec(w2.shape, const),
            pl.BlockSpec(w3.shape, const),
            pl.BlockSpec(w4.shape, const),
        ],
        out_specs=[pl.BlockSpec((tm, o.shape[1]), row) for o in outs],
        out_shape=outs,
        compiler_params=_params(("parallel",)),
        name="in_proj",
    )(x2d, shift, scale, nw, w1, w2, w3, w4)


def _hgrn_body(lbl_ref, qa_ref, fa_ref, ia_ref, ga_ref, nw_ref, o_ref, *, layer, seq_len):
    C, R = HGRN_CHUNK, HGRN_SUB
    lg = lbl_ref[...]
    ex = jnp.exp(lg - jnp.max(lg, axis=0, keepdims=True))
    sm = ex / jnp.sum(ex, axis=0, keepdims=True)
    lb = jnp.zeros((1, LANES), F32)
    for li in range(1, layer + 1):
        lb = lb + sm[li]
    log_lb = jnp.log(lb)
    log_1m = jnp.log1p(-lb)
    nw = nw_ref[...]

    rowi = lax.broadcasted_iota(I32, (C, LANES), 0)
    ri2 = lax.broadcasted_iota(I32, (C, C), 0)
    ci2 = lax.broadcasted_iota(I32, (C, C), 1)
    ltri = jnp.where(ci2 <= ri2, 1.0, 0.0).astype(BF16)
    ones_bf = jnp.ones((LANES, LANES), BF16)
    dlt = ri2 - ci2

    def chunk(ci, st):
        r0 = pl.multiple_of(ci * C, C)
        qa = qa_ref[0, pl.ds(r0, C), :]
        z = fa_ref[0, pl.ds(r0, C), :]
        v = ia_ref[0, pl.ds(r0, C), :]
        ga = ga_ref[0, pl.ds(r0, C), :]
        q = qa * _sigmoid(qa)
        ls = jnp.minimum(z, 0.0) - jnp.log1p(jnp.exp(-jnp.abs(z)))
        t2 = log_1m + ls
        mx = jnp.maximum(log_lb, t2)
        g = mx + jnp.log(jnp.exp(log_lb - mx) + jnp.exp(t2 - mx))
        k = (1.0 - lb) * _sigmoid(-z)
        g1, g2, g3 = _split3(g)
        b = _dot(ltri, g1) + (_dot(ltri, g2) + _dot(ltri, g3))

        a = jnp.zeros((C, C), F32)
        half = C // 2
        while half >= R:
            span = 2 * half
            nb = C // span
            refb = jnp.concatenate(
                [jnp.broadcast_to(b[p * span + half - 1:p * span + half, :], (span, LANES)) for p in range(nb)],
                axis=0)
            second = (rowi & (span - 1)) >= half
            qt = q * jnp.exp(jnp.where(second, b - refb, NEG_INF))
            kt = k * jnp.exp(jnp.where(second, NEG_INF, refb - b))
            al = _dot_nt(qt.astype(BF16), kt.astype(BF16))
            sh = int(math.log2(span))
            a = a + jnp.where((ri2 >> sh) == (ci2 >> sh), al, 0.0)
            half //= 2

        ps = []
        for dl in range(R):
            if dl == 0:
                kd, bd = k, b
            else:
                kd = pltpu.roll(k, dl, 0)
                bd = pltpu.roll(b, dl, 0)
            arg = jnp.where((rowi & (R - 1)) >= dl, b - bd, NEG_INF)
            ps.append((q * kd * jnp.exp(arg)).astype(BF16))
        rs = _dot(jnp.concatenate(ps, axis=0), ones_bf)
        ad = jnp.zeros((C, C), F32)
        for dl in range(R):
            ad = jnp.where(dlt == dl, rs[dl * C:(dl + 1) * C, :], ad)
        shr = int(math.log2(R))
        a = a + jnp.where((ri2 >> shr) == (ci2 >> shr), ad, 0.0)

        vb = v.astype(BF16)
        o = _dot(a.astype(BF16), vb) + _dot_nt((q * jnp.exp(b)).astype(BF16), st.astype(BF16))
        bl = b[C - 1:C, :]
        kdec = (k * jnp.exp(bl - b)).astype(BF16)
        st_new = st * jnp.exp(bl) + _dot_tn(vb, kdec)
        on = o * lax.rsqrt(jnp.mean(o * o, axis=-1, keepdims=True) + NORM_EPS) * nw
        o_ref[0, pl.ds(r0, C), :] = (on * (ga * _sigmoid(ga))).astype(o_ref.dtype)
        return st_new

    lax.fori_loop(0, seq_len // C, chunk, jnp.zeros((LANES, LANES), F32))


def _hgrn(zh, lb_logits, norm_w, layer, bsz, seq_len):
    depth = lb_logits.shape[0]
    hw = HGRN_HEADS * HGRN_DK
    z3 = zh.reshape(bsz, seq_len, zh.shape[1])
    blk = lambda off: pl.BlockSpec((1, seq_len, LANES), lambda b, h: (b, 0, off + h))
    return pl.pallas_call(
        functools.partial(_hgrn_body, layer=layer, seq_len=seq_len),
        grid=(bsz, HGRN_HEADS),
        in_specs=[
            pl.BlockSpec((depth, 1, LANES), lambda b, h: (0, 0, h)),
            blk(0), blk(HGRN_HEADS), blk(2 * HGRN_HEADS), blk(3 * HGRN_HEADS),
            pl.BlockSpec((1, LANES), lambda b, h: (0, h)),
        ],
        out_specs=pl.BlockSpec((1, seq_len, LANES), lambda b, h: (b, 0, h)),
        out_shape=jax.ShapeDtypeStruct((bsz, seq_len, hw), BF16),
        compiler_params=_params(("parallel", "parallel")),
        name="hgrn2",
    )(lb_logits.reshape(depth, 1, hw), z3, z3, z3, z3, norm_w.reshape(1, hw))


def _float_key(x):
    x = jnp.where(x == 0.0, 0.0, x)
    bits = lax.bitcast_convert_type(x, I32)
    return bits ^ ((bits >> 31) & 0x7FFFFFFF)


def _dsa_body(zq_ref, zsq_ref, zs_ref, bias_ref, o_ref,
              key_ref, sel_ref, k2_ref, ik2_ref, v2t_ref, acc_ref, *, k_sel):
    i = pl.program_id(1)
    tq, kb = Q_ROWS, KEY_BLOCK
    sub = tq // kb
    nkb = key_ref.shape[0]
    aw = ATT_HEADS * ATT_DH
    npair = ATT_HEADS // 2
    half = LANES // 2
    nb = sub * (i + 1)

    @pl.when(i == 0)
    def _():
        lo = lax.broadcasted_iota(I32, (kb, LANES), 1) < half
        one = jnp.where(lo, 1.0, 0.0)

        def build(j, c):
            r = pl.multiple_of(j * kb, kb)
            kv = zs_ref[pl.ds(r, kb), 0:LANES]
            ix = zs_ref[pl.ds(r, kb), LANES:2 * LANES]
            vk = pltpu.roll(kv, half, 1)
            xi = pltpu.roll(ix, half, 1)
            sc = ATT_DH ** -0.5
            k2_ref[j] = jnp.concatenate([jnp.where(lo, kv * sc, 0.0), jnp.where(lo, 0.0, vk * sc)], axis=0).astype(BF16)
            ik2_ref[j] = jnp.concatenate([jnp.where(lo, ix, 0.0), jnp.where(lo, 0.0, xi)], axis=0).astype(BF16)
            top = jnp.concatenate([jnp.where(lo, vk, 0.0), one], axis=1)
            bot = jnp.concatenate([jnp.where(lo, 0.0, kv), 1.0 - one], axis=1)
            v2t_ref[j] = jnp.concatenate([top, bot], axis=0).T.astype(BF16)
            return c

        lax.fori_loop(0, nkb, build, 0)

    krow = lax.broadcasted_iota(I32, (kb, tq), 0)
    qcol = lax.broadcasted_iota(I32, (kb, tq), 1)
    r1 = lax.broadcasted_iota(I32, (LANES, LANES), 0)
    c1 = lax.broadcasted_iota(I32, (LANES, LANES), 1)
    slt = jnp.where(c1 < r1, 1.0, 0.0).astype(BF16)
    iwt = (zsq_ref[:, LANES:2 * LANES] * (IDX_HEADS ** -0.5 * IDX_DH ** -0.5)).T

    def causal(j):
        return (kb * j + krow) <= (tq * i + qcol)

    def score_blk(j, c):
        acc = jnp.zeros((kb, tq), F32)
        for p in range(IDX_HEADS // 2):
            s = _dot_nt(ik2_ref[j], zq_ref[:, aw + p * LANES:aw + (p + 1) * LANES])
            w0 = iwt[half + 2 * p:half + 2 * p + 1, :]
            w1 = iwt[half + 2 * p + 1:half + 2 * p + 2, :]
            acc = acc + (w0 * jnp.maximum(s[:kb], 0.0) + w1 * jnp.maximum(s[kb:], 0.0))
        key_ref[j] = _float_key(jnp.where(causal(j), acc, NEG_INF))
        return c

    lax.fori_loop(0, nb, score_blk, 0)

    kf = float(k_sel)

    def count(pred):
        def blk(j, acc):
            hit = jnp.where(pred(key_ref[j]), 1.0, 0.0)
            return acc + jnp.sum(hit.reshape(kb // 8, 8, tq), axis=0)
        acc = lax.fori_loop(0, nb, blk, jnp.zeros((8, tq), F32))
        return jnp.sum(acc, axis=0, keepdims=True)

    zero_i = jnp.zeros((1, tq), I32)
    c0 = count(lambda kj: kj >= zero_i)
    thr0 = jnp.where(c0 >= kf, zero_i, INT_MIN)

    def unsettled(st):
        it, _, done = st
        return jnp.logical_and(it < 31, jnp.min(done) < 0.5)

    def bit_step(st):
        it, thr, done = st
        cand = thr | jnp.left_shift(1, 30 - it)
        cnt = count(lambda kj: kj >= cand)
        return it + 1, jnp.where(cnt >= kf, cand, thr), jnp.maximum(done, jnp.where(cnt == kf, 1.0, 0.0))

    _, thr, _ = lax.while_loop(unsettled, bit_step, (jnp.int32(0), thr0, jnp.where(c0 == kf, 1.0, 0.0)))
    need = kf - count(lambda kj: kj > thr)

    def sel_blk(j, carry):
        kj = key_ref[j]
        eq = jnp.where(kj == thr, 1.0, 0.0)
        pref = _dot(slt, eq.astype(BF16)) + carry
        tie_ok = jnp.where(kj == thr, jnp.where(pref < need, 1.0, 0.0), 0.0)
        chosen = jnp.where(kj > thr, 1.0, tie_ok)
        chosen = jnp.where(causal(j), chosen, 0.0)
        sel_ref[j] = jnp.where(chosen > 0.0, 0.0, NEG_INF)
        return carry + jnp.sum(eq, axis=0, keepdims=True)

    lax.fori_loop(0, nb, sel_blk, jnp.zeros((1, tq), F32))

    def logits(j, p):
        bias = jnp.concatenate([bias_ref[jnp.clip(sub * i + u - j, 0, 2), p] for u in range(sub)], axis=1)
        sel = sel_ref[j]
        s = _dot_nt(k2_ref[j], zq_ref[:, p * LANES:(p + 1) * LANES])
        return s + bias + jnp.concatenate([sel, sel], axis=0)

    def col_max(x):
        return jnp.max(jnp.max(x.reshape(kb // 8, 8, tq), axis=0), axis=0, keepdims=True)

    acc_ref[...] = jnp.zeros(acc_ref.shape, F32)
    quarter = kb // 2

    def att_blk(j, ms):
        out = []
        for p in range(npair):
            s = logits(j, p)
            pes, als, mn = [], [], []
            for u, sl in enumerate((s[:kb], s[kb:])):
                m_old = ms[2 * p + u]
                m_new = jnp.maximum(m_old, col_max(sl))
                m_safe = jnp.where(m_new == NEG_INF, 0.0, m_new)
                pes.append(jnp.exp(sl - m_safe))
                als.append(jnp.broadcast_to(jnp.exp(m_old - m_safe), (quarter, tq)))
                mn.append(m_new)
            pe = jnp.concatenate(pes, axis=0).astype(BF16)
            alpha = jnp.concatenate(als + als, axis=0)
            acc_ref[p] = acc_ref[p] * alpha + _dot(v2t_ref[j], pe)
            out += mn
        return tuple(out)

    lax.fori_loop(0, nb, att_blk, tuple(jnp.full((1, tq), NEG_INF, F32) for _ in range(ATT_HEADS)))
    for p in range(npair):
        a = acc_ref[p]
        o_ref[:, p * LANES:(p + 1) * LANES] = (a[:kb] / a[kb:]).T.astype(o_ref.dtype)


def _t5_bucket(rel):
    max_exact = REL_BUCKETS // 2
    rel_f = jnp.maximum(rel, 1).astype(F32)
    large = max_exact + (jnp.log(rel_f / max_exact) / math.log(REL_MAX_DIST / max_exact)
                         * (REL_BUCKETS - max_exact)).astype(I32)
    large = jnp.minimum(large, REL_BUCKETS - 1)
    return jnp.where(rel < max_exact, rel, large)


def _bias_tiles(rel_bias):
    kb = KEY_BLOCK
    assert kb >= REL_MAX_DIST
    tq = jnp.arange(kb)[:, None]
    sk = jnp.arange(kb)[None, :]
    tiles = []
    for off in range(3):
        rel = jnp.maximum(off * kb + tq - sk, 0)
        per_head = jnp.transpose(rel_bias[_t5_bucket(rel)], (2, 1, 0))
        tiles.append(jnp.concatenate([per_head[0::2], per_head[1::2]], axis=-2))
    return jnp.stack(tiles, axis=0).astype(F32)


def _dsa(zq, zs, bias_tiles, bsz, seq_len):
    t = zq.shape[0]
    nq = seq_len // Q_ROWS
    nkb = seq_len // KEY_BLOCK
    k_sel = min(TOPK_MAX, seq_len // 4)
    aw = ATT_HEADS * ATT_DH
    npair = ATT_HEADS // 2
    return pl.pallas_call(
        functools.partial(_dsa_body, k_sel=k_sel),
        grid=(bsz, nq),
        in_specs=[
            pl.BlockSpec((Q_ROWS, zq.shape[1]), lambda b, i: (b * nq + i, 0)),
            pl.BlockSpec((Q_ROWS, zs.shape[1]), lambda b, i: (b * nq + i, 0)),
            pl.BlockSpec((seq_len, zs.shape[1]), lambda b, i: (b, 0)),
            pl.BlockSpec(bias_tiles.shape, lambda b, i: (0, 0, 0, 0)),
        ],
        out_specs=pl.BlockSpec((Q_ROWS, aw), lambda b, i: (b * nq + i, 0)),
        out_shape=jax.ShapeDtypeStruct((t, aw), BF16),
        scratch_shapes=[
            pltpu.VMEM((nkb, KEY_BLOCK, Q_ROWS), I32),
            pltpu.VMEM((nkb, KEY_BLOCK, Q_ROWS), F32),
            pltpu.VMEM((nkb, 2 * KEY_BLOCK, LANES), BF16),
            pltpu.VMEM((nkb, 2 * KEY_BLOCK, LANES), BF16),
            pltpu.VMEM((nkb, 2 * LANES, 2 * KEY_BLOCK), BF16),
            pltpu.VMEM((npair, 2 * LANES, Q_ROWS), F32),
        ],
        compiler_params=_params(("arbitrary", "arbitrary")),
        name="dsa",
    )(zq, zs, zs, bias_tiles)


def _merge_body(x_ref, oa_ref, ob_ref, zg_ref, gm_ref, shf_ref, scf_ref, nw_ref,
                wpa, wpb, wout, wrt, br, x1_ref, h2_ref, lgt_ref):
    d = x_ref.shape[1]
    ya = _dot(oa_ref[...], wpa[...])
    yb = _dot(ob_ref[...], wpb[...])
    merged = _sigmoid(zg_ref[:, :d]) * ya + _sigmoid(zg_ref[:, d:]) * yb
    y = _dot(merged.astype(BF16), wout[...])
    x1 = x_ref[...] + gm_ref[0] * y
    x1_ref[...] = x1
    h2 = _rms_mod(x1, nw_ref[...], scf_ref[0], shf_ref[0])
    h2_ref[...] = h2
    hh, hl = _split2(h2)
    wh, wl = _split2(wrt[...])
    lgt_ref[...] = _dot_nt(wh, hh) + (_dot_nt(wh, hl) + _dot_nt(wl, hh)) + br[...]


def _merge(x2d, oa, ob, zg, gate_m, shift_f, scale_f, nw, wpa, wpb, wout, wrt, br, seq_len):
    t, d = x2d.shape
    tm = 256
    per_b = seq_len // tm
    e = wrt.shape[0]
    row = lambda i: (i, 0)
    const = lambda i: (0, 0)
    bmap = lambda i: (i // per_b, 0, 0)
    return pl.pallas_call(
        _merge_body,
        grid=(t // tm,),
        in_specs=[
            pl.BlockSpec((tm, d), row),
            pl.BlockSpec((tm, oa.shape[1]), row),
            pl.BlockSpec((tm, ob.shape[1]), row),
            pl.BlockSpec((tm, zg.shape[1]), row),
            pl.BlockSpec((1, 1, d), bmap),
            pl.BlockSpec((1, 1, d), bmap),
            pl.BlockSpec((1, 1, d), bmap),
            pl.BlockSpec((1, d), const),
            pl.BlockSpec(wpa.shape, const),
            pl.BlockSpec(wpb.shape, const),
            pl.BlockSpec(wout.shape, const),
            pl.BlockSpec(wrt.shape, const),
            pl.BlockSpec(br.shape, const),
        ],
        out_specs=[pl.BlockSpec((tm, d), row), pl.BlockSpec((tm, d), row),
                   pl.BlockSpec((e, tm), lambda i: (0, i))],
        out_shape=[jax.ShapeDtypeStruct((t, d), F32), jax.ShapeDtypeStruct((t, d), F32),
                   jax.ShapeDtypeStruct((e, t), F32)],
        compiler_params=_params(("parallel",)),
        name="merge",
    )(x2d, oa, ob, zg, gate_m, shift_f, scale_f, nw, wpa, wpb, wout, wrt, br)


def _route1_body(lg_ref, er_ref, gt_ref, cnt_ref, base_ref):
    i = pl.program_id(0)
    e, tr = lg_ref.shape

    @pl.when(i == 0)
    def _():
        base_ref[...] = jnp.zeros_like(base_ref)

    l = lg_ref[...]
    eidx = lax.broadcasted_iota(I32, (e, tr), 0).astype(F32)
    vals, es, ohs = [], [], []
    for _ in range(TOP_K):
        m = jnp.max(l, axis=0, keepdims=True)
        ek = jnp.min(jnp.where(l == m, eidx, float(e)), axis=0, keepdims=True)
        oh = eidx == ek
        l = jnp.where(oh, NEG_INF, l)
        vals.append(m)
        es.append(ek)
        ohs.append(oh)
    ps = [jnp.exp(v - vals[0]) for v in vals]
    den = ps[0] + ps[1] + ps[2] + ps[3]
    gs = [p / den for p in ps]

    member = jnp.zeros((e, tr), F32)
    for oh in ohs:
        member = member + jnp.where(oh, 1.0, 0.0)
    mb = member.astype(BF16)
    r2 = lax.broadcasted_iota(I32, (tr, tr), 0)
    c2 = lax.broadcasted_iota(I32, (tr, tr), 1)
    sut = jnp.where(r2 < c2, 1.0, 0.0).astype(BF16)
    base = base_ref[...]
    pref = _dot(mb, sut) + jnp.concatenate([base] * (tr // LANES), axis=1)
    ranks = [jnp.sum(jnp.where(oh, pref, 0.0), axis=0, keepdims=True) for oh in ohs]
    er_ref[...] = jnp.concatenate(es + ranks, axis=0).astype(I32)

    row8 = lax.broadcasted_iota(I32, (8, tr), 0)
    g8 = jnp.zeros((8, tr), F32)
    for kk in range(TOP_K):
        g8 = jnp.where(row8 == kk, gs[kk], g8)
    gfull = jnp.concatenate([g8, jnp.zeros((LANES - 8, tr), F32)], axis=0)
    gt_ref[...] = gfull.T

    new_base = base + _dot(mb, jnp.ones((tr, LANES), BF16))
    base_ref[...] = new_base
    cnt_ref[...] = new_base


def _route1(lgt):
    e, t = lgt.shape
    tr = 512
    return pl.pallas_call(
        _route1_body,
        grid=(t // tr,),
        in_specs=[pl.BlockSpec((e, tr), lambda i: (0, i))],
        out_specs=[pl.BlockSpec((2 * TOP_K, tr), lambda i: (0, i)),
                   pl.BlockSpec((tr, LANES), lambda i: (i, 0)),
                   pl.BlockSpec((e, LANES), lambda i: (0, 0))],
        out_shape=[jax.ShapeDtypeStruct((2 * TOP_K, t), I32),
                   jax.ShapeDtypeStruct((t, LANES), F32),
                   jax.ShapeDtypeStruct((e, LANES), F32)],
        scratch_shapes=[pltpu.VMEM((e, LANES), F32)],
        compiler_params=_params(("arbitrary",)),
        name="route_topk",
    )(lgt)


def _route2_body(er_ref, cnt_ref, slot_ref, be_ref, tot_ref):
    e = cnt_ref.shape[0]
    tt = er_ref.shape[1]
    nbp = be_ref.shape[1]
    sh = int(math.log2(MOE_BLOCK))
    cnt = cnt_ref[...].astype(I32)
    padded = ((cnt + (MOE_BLOCK - 1)) >> sh) << sh
    rows = []
    acc = jnp.zeros((1, LANES), I32)
    for ei in range(e):
        rows.append(acc)
        acc = acc + padded[ei:ei + 1, :]
    pstart = jnp.concatenate(rows, axis=0)
    pend = pstart + padded
    tot_ref[...] = jnp.broadcast_to(acc, tot_ref.shape)

    er = er_ref[...]
    eidx = lax.broadcasted_iota(I32, (e, tt), 0)
    pst = jnp.concatenate([pstart] * (tt // LANES), axis=1)
    outs = []
    for kk in range(TOP_K):
        ps = jnp.sum(jnp.where(eidx == er[kk:kk + 1, :], pst, 0), axis=0, keepdims=True)
        outs.append(ps + er[TOP_K + kk:TOP_K + kk + 1, :])
    slot_ref[...] = jnp.concatenate(outs + [jnp.zeros((TOP_K, tt), I32)], axis=0)

    blk0 = lax.broadcasted_iota(I32, (e, nbp), 1) * MOE_BLOCK
    pe = jnp.concatenate([pend] * (nbp // LANES), axis=1)
    nle = jnp.sum(jnp.where(pe <= blk0, 1, 0), axis=0, keepdims=True)
    be_ref[...] = jnp.broadcast_to(jnp.minimum(nle, e - 1), be_ref.shape)


def _route2(er, cnt, nbp):
    t = er.shape[1]
    tt = min(2048, t)
    e = cnt.shape[0]
    return pl.pallas_call(
        _route2_body,
        grid=(t // tt,),
        in_specs=[pl.BlockSpec((2 * TOP_K, tt), lambda i: (0, i)),
                  pl.BlockSpec((e, LANES), lambda i: (0, 0))],
        out_specs=[pl.BlockSpec((2 * TOP_K, tt), lambda i: (0, i)),
                   pl.BlockSpec((8, nbp), lambda i: (0, 0)),
                   pl.BlockSpec((8, LANES), lambda i: (0, 0))],
        out_shape=[jax.ShapeDtypeStruct((2 * TOP_K, t), I32),
                   jax.ShapeDtypeStruct((8, nbp), I32),
                   jax.ShapeDtypeStruct((8, LANES), I32)],
        compiler_params=_params(("arbitrary",)),
        name="route_slots",
    )(er, cnt)


def _dispatch_body(slot_ref, h_ref, xs_in, xs_ref, sem):
    del xs_in
    td = slot_ref.shape[1]

    def issue(tok, c):
        for kk in range(TOP_K):
            pltpu.make_async_copy(h_ref.at[pl.ds(tok, 1)],
                                  xs_ref.at[pl.ds(slot_ref[kk, tok], 1)], sem).start()
        return c

    lax.fori_loop(0, td, issue, 0)
    for kk in range(TOP_K):
        pltpu.make_async_copy(h_ref, xs_ref.at[pl.ds(0, td)], sem).wait()


def _dispatch(slot, h2, n_slots):
    t, d = h2.shape
    td = 512
    xs0 = jnp.zeros((n_slots, d), h2.dtype)
    return pl.pallas_call(
        _dispatch_body,
        grid=(t // td,),
        in_specs=[pl.BlockSpec((2 * TOP_K, td), lambda i: (0, i), memory_space=pltpu.SMEM),
                  pl.BlockSpec((td, d), lambda i: (i, 0)),
                  pl.BlockSpec(memory_space=pl.ANY)],
        out_specs=pl.BlockSpec(memory_space=pl.ANY),
        out_shape=jax.ShapeDtypeStruct((n_slots, d), h2.dtype),
        scratch_shapes=[pltpu.SemaphoreType.DMA],
        input_output_aliases={2: 0},
        compiler_params=_params(("arbitrary",)),
        name="moe_dispatch",
    )(slot, h2, xs0)


def _expert_body(be_ref, nu_ref, xs_ref, w1_ref, b1_ref, w2_ref, b2_ref, y_ref, w1b, w2b):
    i = pl.program_id(0)
    f = w2_ref.shape[1]

    @pl.when(jnp.logical_or(i == 0, be_ref[i] != be_ref[jnp.maximum(i - 1, 0)]))
    def _():
        w1b[...] = w1_ref[0].astype(BF16)
        w2b[...] = w2_ref[0].astype(BF16)

    @pl.when(i < nu_ref[0])
    def _():
        x = xs_ref[...].astype(BF16)
        hm = _dot(x, w1b[...]) + b1_ref[0]
        glu = jnp.minimum(hm[:, :f], SWIGLU_LIMIT)
        lin = jnp.clip(hm[:, f:], -SWIGLU_LIMIT, SWIGLU_LIMIT)
        act = glu * _sigmoid(SWIGLU_ALPHA * glu) * (lin + 1.0)
        y_ref[...] = _dot(act.astype(BF16), w2b[...]) + b2_ref[0]

    @pl.when(i >= nu_ref[0])
    def _():
        y_ref[...] = jnp.zeros_like(y_ref)


def _experts(be, nused, xs, w1, b1, w2, b2):
    ns, d = xs.shape
    e, _, f2 = w1.shape
    f = w2.shape[1]
    nblk = ns // MOE_BLOCK
    grid_spec = pltpu.PrefetchScalarGridSpec(
        num_scalar_prefetch=2,
        grid=(nblk,),
        in_specs=[
            pl.BlockSpec((MOE_BLOCK, d), lambda i, be, nu: (i, 0)),
            pl.BlockSpec((1, d, f2), lambda i, be, nu: (be[i], 0, 0)),
            pl.BlockSpec((1, 1, f2), lambda i, be, nu: (be[i], 0, 0)),
            pl.BlockSpec((1, f, d), lambda i, be, nu: (be[i], 0, 0)),
            pl.BlockSpec((1, 1, d), lambda i, be, nu: (be[i], 0, 0)),
        ],
        out_specs=pl.BlockSpec((MOE_BLOCK, d), lambda i, be, nu: (i, 0)),
        scratch_shapes=[pltpu.VMEM((d, f2), BF16), pltpu.VMEM((f, d), BF16)],
    )
    return pl.pallas_call(
        _expert_body,
        grid_spec=grid_spec,
        out_shape=jax.ShapeDtypeStruct((ns, d), F32),
        compiler_params=_params(("arbitrary",)),
        name="moe_experts",
    )(be, nused, xs, w1, b1.reshape(e, 1, f2), w2, b2.reshape(e, 1, d))


def _combine_body(slot_ref, y_ref, x1_ref, gt_ref, gf_ref, fw_ref, o_ref, buf, sem, *, final):
    td = x1_ref.shape[0]

    def issue(tok, c):
        for kk in range(TOP_K):
            pltpu.make_async_copy(y_ref.at[pl.ds(slot_ref[kk, tok], 1)],
                                  buf.at[kk, pl.ds(tok, 1)], sem).start()
        return c

    lax.fori_loop(0, td, issue, 0)
    for kk in range(TOP_K):
        pltpu.make_async_copy(y_ref.at[pl.ds(0, td)], buf.at[kk], sem).wait()
    g = gt_ref[...]
    y = g[:, 0:1] * buf[0]
    for kk in range(1, TOP_K):
        y = y + g[:, kk:kk + 1] * buf[kk]
    x2 = x1_ref[...] + gf_ref[0] * y
    if final:
        ms = jnp.mean(x2 * x2, axis=-1, keepdims=True)
        x2 = x2 * lax.rsqrt(ms + NORM_EPS) * fw_ref[...]
    o_ref[...] = x2


def _combine(slot, ys, x1, gt, gate_f, fw, seq_len, final):
    t, d = x1.shape
    td = 256
    per_b = seq_len // td
    return pl.pallas_call(
        functools.partial(_combine_body, final=final),
        grid=(t // td,),
        in_specs=[pl.BlockSpec((2 * TOP_K, td), lambda i: (0, i), memory_space=pltpu.SMEM),
                  pl.BlockSpec(memory_space=pl.ANY),
                  pl.BlockSpec((td, d), lambda i: (i, 0)),
                  pl.BlockSpec((td, LANES), lambda i: (i, 0)),
                  pl.BlockSpec((1, 1, d), lambda i: (i // per_b, 0, 0)),
                  pl.BlockSpec((1, d), lambda i: (0, 0))],
        out_specs=pl.BlockSpec((td, d), lambda i: (i, 0)),
        out_shape=jax.ShapeDtypeStruct((t, d), F32),
        scratch_shapes=[pltpu.VMEM((TOP_K, td, d), F32), pltpu.SemaphoreType.DMA],
        compiler_params=_params(("arbitrary",)),
        name="moe_combine",
    )(slot, ys, x1, gt, gate_f, fw)


def _split_w_in(w):
    hw = HGRN_HEADS * HGRN_DK
    aw = ATT_HEADS * ATT_DH
    iw = IDX_HEADS * IDX_DH
    d = w.shape[0]
    o = 4 * hw
    qb = w[:, o:o + aw]
    kb = w[:, o + aw:o + aw + ATT_DH]
    vb = w[:, o + aw + ATT_DH:o + aw + 2 * ATT_DH]
    o2 = o + aw + 2 * ATT_DH
    iq = w[:, o2:o2 + iw]
    ik = w[:, o2 + iw:o2 + iw + IDX_DH]
    ih = w[:, o2 + iw + IDX_DH:o2 + iw + IDX_DH + IDX_HEADS]
    o3 = o2 + iw + IDX_DH + IDX_HEADS
    small = jnp.concatenate([kb, vb, ik, ih, jnp.zeros((d, 2 * LANES - 3 * ATT_DH - IDX_HEADS), w.dtype)], axis=1)
    groups = (w[:, :o], jnp.concatenate([qb, iq], axis=1), small, w[:, o3:])
    return tuple(g.astype(BF16) for g in groups)


def kernel(x, c, w_ada, b_ada, norm_mix_w, w_in, hgrn_lb_logits, hgrn_norm_w, rel_bias, w_proj_a, w_proj_b,
           w_out, norm_ffn_w, w_router, b_router, w_mlp1, b_mlp1, w_mlp2, b_mlp2, final_norm_w):
    bsz, seq_len, d = x.shape
    depth = w_in.shape[0]
    n_exp = w_router.shape[2]
    t = bsz * seq_len
    n_slots = t * TOP_K + n_exp * MOE_BLOCK
    nblk = n_slots // MOE_BLOCK
    nbp = -(-nblk // LANES) * LANES

    mod = _ada(c, w_ada, b_ada)
    bias_tiles = _bias_tiles(rel_bias)
    x2d = x.reshape(t, d)
    for l in range(depth):
        m6 = mod[l].reshape(bsz, N_MOD, 1, d)
        shift_m, scale_m, gate_m, shift_f, scale_f, gate_f = (m6[:, n] for n in range(N_MOD))
        zh, zq, zs, zg = _inproj(x2d, shift_m, scale_m, norm_mix_w[l].reshape(1, d), _split_w_in(w_in[l]), seq_len)
        oa = _hgrn(zh, hgrn_lb_logits, hgrn_norm_w[l], l, bsz, seq_len).reshape(t, -1)
        ob = _dsa(zq, zs, bias_tiles, bsz, seq_len)
        x1, h2, lgt = _merge(x2d, oa, ob, zg, gate_m, shift_f, scale_f, norm_ffn_w[l].reshape(1, d),
                             w_proj_a[l].astype(BF16), w_proj_b[l].astype(BF16), w_out[l].astype(BF16),
                             w_router[l].T, b_router[l].reshape(n_exp, 1), seq_len)
        er, gt, cnt = _route1(lgt)
        slot, be, tot = _route2(er, cnt, nbp)
        xs = _dispatch(slot, h2, n_slots)
        nused = (tot[0, :1] >> int(math.log2(MOE_BLOCK))).astype(I32)
        ys = _experts(be[0, :nblk], nused, xs, w_mlp1[l], b_mlp1[l], w_mlp2[l], b_mlp2[l])
        x2d = _combine(slot, ys, x1, gt, gate_f, final_norm_w.reshape(1, d), seq_len, final=(l == depth - 1))
    return x2d.reshape(bsz, seq_len, d)
```

```python
import functools
import math

import jax
import jax.numpy as jnp
from jax import lax
from jax.experimental import pallas as pl
from jax.experimental.pallas import tpu as pltpu

F32 = jnp.float32
BF16 = jnp.bfloat16
I32 = jnp.int32

HGRN_HEADS = 4
HGRN_DK = 128
ATT_HEADS = 8
ATT_DH = 64
IDX_HEADS = 8
IDX_DH = 64
TOPK_MAX = 256
REL_BUCKETS = 32
REL_MAX_DIST = 128
TOP_K = 4
SWIGLU_ALPHA = 1.702
SWIGLU_LIMIT = 7.0
NORM_EPS = 1e-6
N_MOD = 6

LANES = 128
HGRN_CHUNK = 128
HGRN_SUB = 4
KEY_BLOCK = 128
Q_ROWS = 256
MOE_BLOCK = 256
DMA_UNROLL = 8
TOK_TILE = 8
VMEM_LIMIT = 56 * 1024 * 1024
NEG_INF = float("-inf")
INT_MIN = -(2 ** 31)


def _dot(a, b):
    return jnp.dot(a, b, preferred_element_type=F32)


def _dot_nt(a, b):
    return lax.dot_general(a, b, (((1,), (1,)), ((), ())), preferred_element_type=F32)


def _dot_tn(a, b):
    return lax.dot_general(a, b, (((0,), (0,)), ((), ())), preferred_element_type=F32)


def _split2(a):
    hi = a.astype(BF16)
    lo = (a - hi.astype(F32)).astype(BF16)
    return hi, lo


def _split3(a):
    hi = a.astype(BF16)
    r = a - hi.astype(F32)
    mid = r.astype(BF16)
    lo = (r - mid.astype(F32)).astype(BF16)
    return hi, mid, lo


def _sigmoid(x):
    return 0.5 * jnp.tanh(0.5 * x) + 0.5


def _params(sem):
    return pltpu.CompilerParams(dimension_semantics=sem, vmem_limit_bytes=VMEM_LIMIT)


def _ada_body(c_ref, w_ref, b_ref, o_ref):
    c = c_ref[...]
    ca = c * _sigmoid(c)
    o_ref[0] = _dot(ca, w_ref[0]) + b_ref[0]


def _ada(c, w_ada, b_ada):
    depth, d, nd = w_ada.shape
    bsz = c.shape[0]
    tn = 1024
    return pl.pallas_call(
        _ada_body,
        grid=(depth, nd // tn),
        in_specs=[
            pl.BlockSpec((bsz, d), lambda l, j: (0, 0)),
            pl.BlockSpec((1, d, tn), lambda l, j: (l, 0, j)),
            pl.BlockSpec((1, 1, tn), lambda l, j: (l, 0, j)),
        ],
        out_specs=pl.BlockSpec((1, bsz, tn), lambda l, j: (l, 0, j)),
        out_shape=jax.ShapeDtypeStruct((depth, bsz, nd), F32),
        compiler_params=_params(("parallel", "parallel")),
        name="ada_mod",
    )(c, w_ada, b_ada.reshape(depth, 1, nd))


def _rms_mod(x, nw, scale, shift):
    ms = jnp.mean(x * x, axis=-1, keepdims=True)
    y = x * lax.rsqrt(ms + NORM_EPS) * nw
    return y * (1.0 + scale) + shift


def _inproj_body(x_ref, sh_ref, sc_ref, nw_ref, w1, w2, w3, w4, zh, zq, zs, zg):
    h = _rms_mod(x_ref[...], nw_ref[...], sc_ref[0], sh_ref[0]).astype(BF16)
    zh[...] = _dot(h, w1[...])
    zq[...] = _dot(h, w2[...]).astype(BF16)
    zs[...] = _dot(h, w3[...])
    zg[...] = _dot(h, w4[...])


def _inproj(x2d, shift, scale, nw, ws, seq_len):
    t, d = x2d.shape
    tm = 256
    per_b = seq_len // tm
    w1, w2, w3, w4 = ws
    row = lambda i: (i, 0)
    const = lambda i: (0, 0)
    bmap = lambda i: (i // per_b, 0, 0)
    outs = [
        jax.ShapeDtypeStruct((t, w1.shape[1]), F32),
        jax.ShapeDtypeStruct((t, w2.shape[1]), BF16),
        jax.ShapeDtypeStruct((t, w3.shape[1]), F32),
        jax.ShapeDtypeStruct((t, w4.shape[1]), F32),
    ]
    return pl.pallas_call(
        _inproj_body,
        grid=(t // tm,),
        in_specs=[
            pl.BlockSpec((tm, d), row),
            pl.BlockSpec((1, 1, d), bmap),
            pl.BlockSpec((1, 1, d), bmap),
            pl.BlockSpec((1, d), const),
            pl.BlockSpec(w1.shape, const),
            pl.BlockSpec(w2.shape, const),
            pl.BlockSpec(w3.shape, const),
            pl.BlockSpec(w4.shape, const),
        ],
        out_specs=[pl.BlockSpec((tm, o.shape[1]), row) for o in outs],
        out_shape=outs,
        compiler_params=_params(("parallel",)),
        name="in_proj",
    )(x2d, shift, scale, nw, w1, w2, w3, w4)


def _hgrn_body(lbl_ref, qa_ref, fa_ref, ia_ref, ga_ref, nw_ref, o_ref, *, layer, seq_len):
    C, R = HGRN_CHUNK, HGRN_SUB
    lg = lbl_ref[...]
    ex = jnp.exp(lg - jnp.max(lg, axis=0, keepdims=True))
    sm = ex / jnp.sum(ex, axis=0, keepdims=True)
    lb = jnp.zeros((1, LANES), F32)
    for li in range(1, layer + 1):
        lb = lb + sm[li]
    log_lb = jnp.log(lb)
    log_1m = jnp.log1p(-lb)
    nw = nw_ref[...]

    rowi = lax.broadcasted_iota(I32, (C, LANES), 0)
    ri2 = lax.broadcasted_iota(I32, (C, C), 0)
    ci2 = lax.broadcasted_iota(I32, (C, C), 1)
    ltri = jnp.where(ci2 <= ri2, 1.0, 0.0).astype(BF16)
    ones_bf = jnp.ones((LANES, LANES), BF16)
    dlt = ri2 - ci2

    def chunk(ci, st):
        r0 = pl.multiple_of(ci * C, C)
        qa = qa_ref[0, pl.ds(r0, C), :]
        z = fa_ref[0, pl.ds(r0, C), :]
        v = ia_ref[0, pl.ds(r0, C), :]
        ga = ga_ref[0, pl.ds(r0, C), :]
        q = qa * _sigmoid(qa)
        ls = jnp.minimum(z, 0.0) - jnp.log1p(jnp.exp(-jnp.abs(z)))
        t2 = log_1m + ls
        mx = jnp.maximum(log_lb, t2)
        g = mx + jnp.log(jnp.exp(log_lb - mx) + jnp.exp(t2 - mx))
        k = (1.0 - lb) * _sigmoid(-z)
        g1, g2, g3 = _split3(g)
        b = _dot(ltri, g1) + (_dot(ltri, g2) + _dot(ltri, g3))

        a = jnp.zeros((C, C), F32)
        half = C // 2
        while half >= R:
            span = 2 * half
            nb = C // span
            refb = jnp.concatenate(
                [jnp.broadcast_to(b[p * span + half - 1:p * span + half, :], (span, LANES)) for p in range(nb)],
                axis=0)
            second = (rowi & (span - 1)) >= half
            qt = q * jnp.exp(jnp.where(second, b - refb, NEG_INF))
            kt = k * jnp.exp(jnp.where(second, NEG_INF, refb - b))
            al = _dot_nt(qt.astype(BF16), kt.astype(BF16))
            sh = int(math.log2(span))
            a = a + jnp.where((ri2 >> sh) == (ci2 >> sh), al, 0.0)
            half //= 2

        ps = []
        for dl in range(R):
            if dl == 0:
                kd, bd = k, b
            else:
                kd = pltpu.roll(k, dl, 0)
                bd = pltpu.roll(b, dl, 0)
            arg = jnp.where((rowi & (R - 1)) >= dl, b - bd, NEG_INF)
            ps.append((q * kd * jnp.exp(arg)).astype(BF16))
        rs = _dot(jnp.concatenate(ps, axis=0), ones_bf)
        ad = jnp.zeros((C, C), F32)
        for dl in range(R):
            ad = jnp.where(dlt == dl, rs[dl * C:(dl + 1) * C, :], ad)
        shr = int(math.log2(R))
        a = a + jnp.where((ri2 >> shr) == (ci2 >> shr), ad, 0.0)

        vb = v.astype(BF16)
        o = _dot(a.astype(BF16), vb) + _dot_nt((q * jnp.exp(b)).astype(BF16), st.astype(BF16))
        bl = b[C - 1:C, :]
        kdec = (k * jnp.exp(bl - b)).astype(BF16)
        st_new = st * jnp.exp(bl) + _dot_tn(vb, kdec)
        on = o * lax.rsqrt(jnp.mean(o * o, axis=-1, keepdims=True) + NORM_EPS) * nw
        o_ref[0, pl.ds(r0, C), :] = (on * (ga * _sigmoid(ga))).astype(o_ref.dtype)
        return st_new

    lax.fori_loop(0, seq_len // C, chunk, jnp.zeros((LANES, LANES), F32))


def _hgrn(zh, lb_logits, norm_w, layer, bsz, seq_len):
    depth = lb_logits.shape[0]
    hw = HGRN_HEADS * HGRN_DK
    z3 = zh.reshape(bsz, seq_len, zh.shape[1])
    blk = lambda off: pl.BlockSpec((1, seq_len, LANES), lambda b, h: (b, 0, off + h))
    return pl.pallas_call(
        functools.partial(_hgrn_body, layer=layer, seq_len=seq_len),
        grid=(bsz, HGRN_HEADS),
        in_specs=[
            pl.BlockSpec((depth, 1, LANES), lambda b, h: (0, 0, h)),
            blk(0), blk(HGRN_HEADS), blk(2 * HGRN_HEADS), blk(3 * HGRN_HEADS),
            pl.BlockSpec((1, LANES), lambda b, h: (0, h)),
        ],
        out_specs=pl.BlockSpec((1, seq_len, LANES), lambda b, h: (b, 0, h)),
        out_shape=jax.ShapeDtypeStruct((bsz, seq_len, hw), BF16),
        compiler_params=_params(("parallel", "parallel")),
        name="hgrn2",
    )(lb_logits.reshape(depth, 1, hw), z3, z3, z3, z3, norm_w.reshape(1, hw))


def _float_key(x):
    x = jnp.where(x == 0.0, 0.0, x)
    bits = lax.bitcast_convert_type(x, I32)
    return bits ^ ((bits >> 31) & 0x7FFFFFFF)


def _dsa_body(zq_ref, zsq_ref, zs_ref, bias_ref, o_ref,
              key_ref, sel_ref, k2_ref, ik2_ref, v2t_ref, acc_ref, *, k_sel):
    i = pl.program_id(1)
    tq, kb = Q_ROWS, KEY_BLOCK
    sub = tq // kb
    nkb = key_ref.shape[0]
    aw = ATT_HEADS * ATT_DH
    npair = ATT_HEADS // 2
    half = LANES // 2
    nb = sub * (i + 1)

    @pl.when(i == 0)
    def _():
        lo = lax.broadcasted_iota(I32, (kb, LANES), 1) < half
        one = jnp.where(lo, 1.0, 0.0)

        def build(j, c):
            r = pl.multiple_of(j * kb, kb)
            kv = zs_ref[pl.ds(r, kb), 0:LANES]
            ix = zs_ref[pl.ds(r, kb), LANES:2 * LANES]
            vk = pltpu.roll(kv, half, 1)
            xi = pltpu.roll(ix, half, 1)
            sc = ATT_DH ** -0.5
            k2_ref[j] = jnp.concatenate([jnp.where(lo, kv * sc, 0.0), jnp.where(lo, 0.0, vk * sc)], axis=0).astype(BF16)
            ik2_ref[j] = jnp.concatenate([jnp.where(lo, ix, 0.0), jnp.where(lo, 0.0, xi)], axis=0).astype(BF16)
            top = jnp.concatenate([jnp.where(lo, vk, 0.0), one], axis=1)
            bot = jnp.concatenate([jnp.where(lo, 0.0, kv), 1.0 - one], axis=1)
            v2t_ref[j] = jnp.concatenate([top, bot], axis=0).T.astype(BF16)
            return c

        lax.fori_loop(0, nkb, build, 0)

    krow = lax.broadcasted_iota(I32, (kb, tq), 0)
    qcol = lax.broadcasted_iota(I32, (kb, tq), 1)
    r1 = lax.broadcasted_iota(I32, (LANES, LANES), 0)
    c1 = lax.broadcasted_iota(I32, (LANES, LANES), 1)
    slt = jnp.where(c1 < r1, 1.0, 0.0).astype(BF16)
    iwt = (zsq_ref[:, LANES:2 * LANES] * (IDX_HEADS ** -0.5 * IDX_DH ** -0.5)).T

    def causal(j):
        return (kb * j + krow) <= (tq * i + qcol)

    def score_blk(j, c):
        acc = jnp.zeros((kb, tq), F32)
        for p in range(IDX_HEADS // 2):
            s = _dot_nt(ik2_ref[j], zq_ref[:, aw + p * LANES:aw + (p + 1) * LANES])
            w0 = iwt[half + 2 * p:half + 2 * p + 1, :]
            w1 = iwt[half + 2 * p + 1:half + 2 * p + 2, :]
            acc = acc + (w0 * jnp.maximum(s[:kb], 0.0) + w1 * jnp.maximum(s[kb:], 0.0))
        key_ref[j] = _float_key(jnp.where(causal(j), acc, NEG_INF))
        return c

    lax.fori_loop(0, nb, score_blk, 0)

    kf = float(k_sel)

    def count(pred):
        def blk(j, acc):
            hit = jnp.where(pred(key_ref[j]), 1.0, 0.0)
            return acc + jnp.sum(hit.reshape(kb // 8, 8, tq), axis=0)
        acc = lax.fori_loop(0, nb, blk, jnp.zeros((8, tq), F32))
        return jnp.sum(acc, axis=0, keepdims=True)

    zero_i = jnp.zeros((1, tq), I32)
    thr0 = jnp.where(count(lambda kj: kj >= zero_i) >= kf, zero_i, INT_MIN)

    def bit_step(it, thr):
        cand = thr | jnp.left_shift(1, 30 - it)
        return jnp.where(count(lambda kj: kj >= cand) >= kf, cand, thr)

    thr = lax.fori_loop(0, 31, bit_step, thr0)
    need = kf - count(lambda kj: kj > thr)

    def sel_blk(j, carry):
        kj = key_ref[j]
        eq = jnp.where(kj == thr, 1.0, 0.0)
        pref = _dot(slt, eq.astype(BF16)) + carry
        tie_ok = jnp.where(kj == thr, jnp.where(pref < need, 1.0, 0.0), 0.0)
        chosen = jnp.where(kj > thr, 1.0, tie_ok)
        chosen = jnp.where(causal(j), chosen, 0.0)
        sel_ref[j] = jnp.where(chosen > 0.0, 0.0, NEG_INF)
        return carry + jnp.sum(eq, axis=0, keepdims=True)

    lax.fori_loop(0, nb, sel_blk, jnp.zeros((1, tq), F32))

    def logits(j, p):
        bias = jnp.concatenate([bias_ref[jnp.clip(sub * i + u - j, 0, 2), p] for u in range(sub)], axis=1)
        sel = sel_ref[j]
        s = _dot_nt(k2_ref[j], zq_ref[:, p * LANES:(p + 1) * LANES])
        return s + bias + jnp.concatenate([sel, sel], axis=0)

    def col_max(x):
        return jnp.max(jnp.max(x.reshape(kb // 8, 8, tq), axis=0), axis=0, keepdims=True)

    acc_ref[...] = jnp.zeros(acc_ref.shape, F32)
    quarter = kb // 2

    def att_blk(j, ms):
        out = []
        for p in range(npair):
            s = logits(j, p)
            pes, als, mn = [], [], []
            for u, sl in enumerate((s[:kb], s[kb:])):
                m_old = ms[2 * p + u]
                m_new = jnp.maximum(m_old, col_max(sl))
                m_safe = jnp.where(m_new == NEG_INF, 0.0, m_new)
                pes.append(jnp.exp(sl - m_safe))
                als.append(jnp.broadcast_to(jnp.exp(m_old - m_safe), (quarter, tq)))
                mn.append(m_new)
            pe = jnp.concatenate(pes, axis=0).astype(BF16)
            alpha = jnp.concatenate(als + als, axis=0)
            acc_ref[p] = acc_ref[p] * alpha + _dot(v2t_ref[j], pe)
            out += mn
        return tuple(out)

    lax.fori_loop(0, nb, att_blk, tuple(jnp.full((1, tq), NEG_INF, F32) for _ in range(ATT_HEADS)))
    for p in range(npair):
        a = acc_ref[p]
        o_ref[:, p * LANES:(p + 1) * LANES] = (a[:kb] / a[kb:]).T.astype(o_ref.dtype)


def _t5_bucket(rel):
    max_exact = REL_BUCKETS // 2
    rel_f = jnp.maximum(rel, 1).astype(F32)
    large = max_exact + (jnp.log(rel_f / max_exact) / math.log(REL_MAX_DIST / max_exact)
                         * (REL_BUCKETS - max_exact)).astype(I32)
    large = jnp.minimum(large, REL_BUCKETS - 1)
    return jnp.where(rel < max_exact, rel, large)


def _bias_tiles(rel_bias):
    kb = KEY_BLOCK
    assert kb >= REL_MAX_DIST
    tq = jnp.arange(kb)[:, None]
    sk = jnp.arange(kb)[None, :]
    tiles = []
    for off in range(3):
        rel = jnp.maximum(off * kb + tq - sk, 0)
        per_head = jnp.transpose(rel_bias[_t5_bucket(rel)], (2, 1, 0))
        tiles.append(jnp.concatenate([per_head[0::2], per_head[1::2]], axis=-2))
    return jnp.stack(tiles, axis=0).astype(F32)


def _dsa(zq, zs, bias_tiles, bsz, seq_len):
    t = zq.shape[0]
    nq = seq_len // Q_ROWS
    nkb = seq_len // KEY_BLOCK
    k_sel = min(TOPK_MAX, seq_len // 4)
    aw = ATT_HEADS * ATT_DH
    npair = ATT_HEADS // 2
    return pl.pallas_call(
        functools.partial(_dsa_body, k_sel=k_sel),
        grid=(bsz, nq),
        in_specs=[
            pl.BlockSpec((Q_ROWS, zq.shape[1]), lambda b, i: (b * nq + i, 0)),
            pl.BlockSpec((Q_ROWS, zs.shape[1]), lambda b, i: (b * nq + i, 0)),
            pl.BlockSpec((seq_len, zs.shape[1]), lambda b, i: (b, 0)),
            pl.BlockSpec(bias_tiles.shape, lambda b, i: (0, 0, 0, 0)),
        ],
        out_specs=pl.BlockSpec((Q_ROWS, aw), lambda b, i: (b * nq + i, 0)),
        out_shape=jax.ShapeDtypeStruct((t, aw), BF16),
        scratch_shapes=[
            pltpu.VMEM((nkb, KEY_BLOCK, Q_ROWS), I32),
            pltpu.VMEM((nkb, KEY_BLOCK, Q_ROWS), F32),
            pltpu.VMEM((nkb, 2 * KEY_BLOCK, LANES), BF16),
            pltpu.VMEM((nkb, 2 * KEY_BLOCK, LANES), BF16),
            pltpu.VMEM((nkb, 2 * LANES, 2 * KEY_BLOCK), BF16),
            pltpu.VMEM((npair, 2 * LANES, Q_ROWS), F32),
        ],
        compiler_params=_params(("arbitrary", "arbitrary")),
        name="dsa",
    )(zq, zs, zs, bias_tiles)


def _merge_body(x_ref, oa_ref, ob_ref, zg_ref, gm_ref, shf_ref, scf_ref, nw_ref,
                wpa, wpb, wout, wrt, br, x1_ref, h2_ref, lgt_ref):
    d = x_ref.shape[1]
    ya = _dot(oa_ref[...], wpa[...])
    yb = _dot(ob_ref[...], wpb[...])
    merged = _sigmoid(zg_ref[:, :d]) * ya + _sigmoid(zg_ref[:, d:]) * yb
    y = _dot(merged.astype(BF16), wout[...])
    x1 = x_ref[...] + gm_ref[0] * y
    x1_ref[...] = x1
    h2 = _rms_mod(x1, nw_ref[...], scf_ref[0], shf_ref[0])
    for j in range(TOK_TILE):
        h2_ref[pl.ds(j, x_ref.shape[0], stride=TOK_TILE), :] = h2[:, j * LANES:(j + 1) * LANES]
    hh, hl = _split2(h2)
    wh, wl = _split2(wrt[...])
    lgt_ref[...] = _dot_nt(wh, hh) + (_dot_nt(wh, hl) + _dot_nt(wl, hh)) + br[...]


def _merge(x2d, oa, ob, zg, gate_m, shift_f, scale_f, nw, wpa, wpb, wout, wrt, br, seq_len):
    t, d = x2d.shape
    assert d == TOK_TILE * LANES
    tm = 256
    per_b = seq_len // tm
    e = wrt.shape[0]
    row = lambda i: (i, 0)
    const = lambda i: (0, 0)
    bmap = lambda i: (i // per_b, 0, 0)
    return pl.pallas_call(
        _merge_body,
        grid=(t // tm,),
        in_specs=[
            pl.BlockSpec((tm, d), row),
            pl.BlockSpec((tm, oa.shape[1]), row),
            pl.BlockSpec((tm, ob.shape[1]), row),
            pl.BlockSpec((tm, zg.shape[1]), row),
            pl.BlockSpec((1, 1, d), bmap),
            pl.BlockSpec((1, 1, d), bmap),
            pl.BlockSpec((1, 1, d), bmap),
            pl.BlockSpec((1, d), const),
            pl.BlockSpec(wpa.shape, const),
            pl.BlockSpec(wpb.shape, const),
            pl.BlockSpec(wout.shape, const),
            pl.BlockSpec(wrt.shape, const),
            pl.BlockSpec(br.shape, const),
        ],
        out_specs=[pl.BlockSpec((tm, d), row), pl.BlockSpec((tm * TOK_TILE, LANES), row),
                   pl.BlockSpec((e, tm), lambda i: (0, i))],
        out_shape=[jax.ShapeDtypeStruct((t, d), F32), jax.ShapeDtypeStruct((t * TOK_TILE, LANES), F32),
                   jax.ShapeDtypeStruct((e, t), F32)],
        compiler_params=_params(("parallel",)),
        name="merge",
    )(x2d, oa, ob, zg, gate_m, shift_f, scale_f, nw, wpa, wpb, wout, wrt, br)


def _route1_body(lg_ref, er_ref, gt_ref, cnt_ref, base_ref):
    i = pl.program_id(0)
    e, tr = lg_ref.shape

    @pl.when(i == 0)
    def _():
        base_ref[...] = jnp.zeros_like(base_ref)

    l = lg_ref[...]
    eidx = lax.broadcasted_iota(I32, (e, tr), 0).astype(F32)
    vals, es, ohs = [], [], []
    for _ in range(TOP_K):
        m = jnp.max(l, axis=0, keepdims=True)
        ek = jnp.min(jnp.where(l == m, eidx, float(e)), axis=0, keepdims=True)
        oh = eidx == ek
        l = jnp.where(oh, NEG_INF, l)
        vals.append(m)
        es.append(ek)
        ohs.append(oh)
    ps = [jnp.exp(v - vals[0]) for v in vals]
    den = ps[0] + ps[1] + ps[2] + ps[3]
    gs = [p / den for p in ps]

    member = jnp.zeros((e, tr), F32)
    for oh in ohs:
        member = member + jnp.where(oh, 1.0, 0.0)
    mb = member.astype(BF16)
    r2 = lax.broadcasted_iota(I32, (tr, tr), 0)
    c2 = lax.broadcasted_iota(I32, (tr, tr), 1)
    sut = jnp.where(r2 < c2, 1.0, 0.0).astype(BF16)
    base = base_ref[...]
    pref = _dot(mb, sut) + jnp.concatenate([base] * (tr // LANES), axis=1)
    ranks = [jnp.sum(jnp.where(oh, pref, 0.0), axis=0, keepdims=True) for oh in ohs]
    er_ref[...] = jnp.concatenate(es + ranks, axis=0).astype(I32)

    row8 = lax.broadcasted_iota(I32, (8, tr), 0)
    g8 = jnp.zeros((8, tr), F32)
    for kk in range(TOP_K):
        g8 = jnp.where(row8 == kk, gs[kk], g8)
    gfull = jnp.concatenate([g8, jnp.zeros((LANES - 8, tr), F32)], axis=0)
    gt_ref[...] = gfull.T

    new_base = base + _dot(mb, jnp.ones((tr, LANES), BF16))
    base_ref[...] = new_base
    cnt_ref[...] = new_base


def _route1(lgt):
    e, t = lgt.shape
    tr = 512
    return pl.pallas_call(
        _route1_body,
        grid=(t // tr,),
        in_specs=[pl.BlockSpec((e, tr), lambda i: (0, i))],
        out_specs=[pl.BlockSpec((2 * TOP_K, tr), lambda i: (0, i)),
                   pl.BlockSpec((tr, LANES), lambda i: (i, 0)),
                   pl.BlockSpec((e, LANES), lambda i: (0, 0))],
        out_shape=[jax.ShapeDtypeStruct((2 * TOP_K, t), I32),
                   jax.ShapeDtypeStruct((t, LANES), F32),
                   jax.ShapeDtypeStruct((e, LANES), F32)],
        scratch_shapes=[pltpu.VMEM((e, LANES), F32)],
        compiler_params=_params(("arbitrary",)),
        name="route_topk",
    )(lgt)


def _route2_body(er_ref, cnt_ref, slot_ref, be_ref, tot_ref, fs_ref):
    e = cnt_ref.shape[0]
    tt = er_ref.shape[1]
    nbp = be_ref.shape[1]
    sh = int(math.log2(MOE_BLOCK))
    cnt = cnt_ref[...].astype(I32)
    padded = ((cnt + (MOE_BLOCK - 1)) >> sh) << sh
    rows = []
    acc = jnp.zeros((1, LANES), I32)
    for ei in range(e):
        rows.append(acc)
        acc = acc + padded[ei:ei + 1, :]
    pstart = jnp.concatenate(rows, axis=0)
    pend = pstart + padded
    tot_ref[...] = jnp.broadcast_to(acc, tot_ref.shape)
    fs_ref[...] = jnp.where(padded > 0, pend - MOE_BLOCK, -1)

    er = er_ref[...]
    eidx = lax.broadcasted_iota(I32, (e, tt), 0)
    pst = jnp.concatenate([pstart] * (tt // LANES), axis=1)
    outs = []
    for kk in range(TOP_K):
        ps = jnp.sum(jnp.where(eidx == er[kk:kk + 1, :], pst, 0), axis=0, keepdims=True)
        outs.append(ps + er[TOP_K + kk:TOP_K + kk + 1, :])
    slot_ref[...] = jnp.concatenate(outs + [jnp.zeros((TOP_K, tt), I32)], axis=0)

    blk0 = lax.broadcasted_iota(I32, (e, nbp), 1) * MOE_BLOCK
    pe = jnp.concatenate([pend] * (nbp // LANES), axis=1)
    nle = jnp.sum(jnp.where(pe <= blk0, 1, 0), axis=0, keepdims=True)
    be_ref[...] = jnp.broadcast_to(jnp.minimum(nle, e - 1), be_ref.shape)


def _route2(er, cnt, nbp):
    t = er.shape[1]
    tt = min(2048, t)
    e = cnt.shape[0]
    return pl.pallas_call(
        _route2_body,
        grid=(t // tt,),
        in_specs=[pl.BlockSpec((2 * TOP_K, tt), lambda i: (0, i)),
                  pl.BlockSpec((e, LANES), lambda i: (0, 0))],
        out_specs=[pl.BlockSpec((2 * TOP_K, tt), lambda i: (0, i)),
                   pl.BlockSpec((8, nbp), lambda i: (0, 0)),
                   pl.BlockSpec((8, LANES), lambda i: (0, 0)),
                   pl.BlockSpec((e, LANES), lambda i: (0, 0))],
        out_shape=[jax.ShapeDtypeStruct((2 * TOP_K, t), I32),
                   jax.ShapeDtypeStruct((8, nbp), I32),
                   jax.ShapeDtypeStruct((8, LANES), I32),
                   jax.ShapeDtypeStruct((e, LANES), I32)],
        compiler_params=_params(("arbitrary",)),
        name="route_slots",
    )(er, cnt)


def _tile(row):
    return pl.ds(pl.multiple_of(row * TOK_TILE, TOK_TILE), TOK_TILE)


def _dispatch_body(fs_ref, nu_ref, slot_ref, h_ref, xs_ref, zbuf, sem, zsem):
    i = pl.program_id(0)
    td = slot_ref.shape[1]

    @pl.when(i == 0)
    def _():
        zbuf[...] = jnp.zeros_like(zbuf)
        zrows = zbuf.shape[0]
        nblk = xs_ref.shape[0] // zrows
        fills = []
        for e in range(fs_ref.shape[0]):
            fills.append((fs_ref[e] >= 0, jnp.maximum(fs_ref[e], 0) * TOK_TILE))
        for b in range(fs_ref.shape[0]):
            blk = nu_ref[0] + b
            fills.append((blk < nblk, jnp.minimum(blk, nblk - 1) * zrows))
        for pred, start in fills:
            @pl.when(pred)
            def _(start=start):
                pltpu.make_async_copy(zbuf, xs_ref.at[pl.ds(pl.multiple_of(start, zrows), zrows)], zsem).start()
        for pred, _ in fills:
            @pl.when(pred)
            def _():
                pltpu.make_async_copy(zbuf, xs_ref.at[pl.ds(0, zrows)], zsem).wait()

    def issue(g, c):
        for u in range(DMA_UNROLL):
            tok = g * DMA_UNROLL + u
            for kk in range(TOP_K):
                pltpu.make_async_copy(h_ref.at[_tile(tok)], xs_ref.at[_tile(slot_ref[kk, tok])], sem).start()
        return c

    lax.fori_loop(0, td // DMA_UNROLL, issue, 0)
    for kk in range(TOP_K):
        pltpu.make_async_copy(h_ref, xs_ref.at[pl.ds(0, td * TOK_TILE)], sem).wait()


def _dispatch(fill_start, nused, slot, h2, n_slots):
    t = h2.shape[0] // TOK_TILE
    td = 512
    grid_spec = pltpu.PrefetchScalarGridSpec(
        num_scalar_prefetch=2,
        grid=(t // td,),
        in_specs=[pl.BlockSpec((2 * TOP_K, td), lambda i, fs, nu: (0, i), memory_space=pltpu.SMEM),
                  pl.BlockSpec((td * TOK_TILE, LANES), lambda i, fs, nu: (i, 0))],
        out_specs=pl.BlockSpec(memory_space=pl.ANY),
        scratch_shapes=[pltpu.VMEM((MOE_BLOCK * TOK_TILE, LANES), h2.dtype), pltpu.SemaphoreType.DMA, pltpu.SemaphoreType.DMA],
    )
    return pl.pallas_call(
        _dispatch_body,
        grid_spec=grid_spec,
        out_shape=jax.ShapeDtypeStruct((n_slots * TOK_TILE, LANES), h2.dtype),
        compiler_params=_params(("arbitrary",)),
        name="moe_dispatch",
    )(fill_start, nused, slot, h2)


def _expert_body(be_ref, nu_ref, xs_ref, w1_ref, b1_ref, w2_ref, b2_ref, y_ref, w1b, w2b):
    i = pl.program_id(0)
    f = w2_ref.shape[2]

    @pl.when(jnp.logical_or(i == 0, be_ref[i] != be_ref[jnp.maximum(i - 1, 0)]))
    def _():
        w1b[...] = w1_ref[0, 0].astype(BF16)
        w2b[...] = w2_ref[0, 0].astype(BF16)

    @pl.when(i < nu_ref[0])
    def _():
        x = jnp.concatenate([xs_ref[pl.ds(j, MOE_BLOCK, stride=TOK_TILE), :] for j in range(TOK_TILE)], axis=1)
        hm = _dot(x.astype(BF16), w1b[...]) + b1_ref[0, 0]
        glu = jnp.minimum(hm[:, :f], SWIGLU_LIMIT)
        lin = jnp.clip(hm[:, f:], -SWIGLU_LIMIT, SWIGLU_LIMIT)
        act = glu * _sigmoid(SWIGLU_ALPHA * glu) * (lin + 1.0)
        y = _dot(act.astype(BF16), w2b[...]) + b2_ref[0, 0]
        for j in range(TOK_TILE):
            y_ref[pl.ds(j, MOE_BLOCK, stride=TOK_TILE), :] = y[:, j * LANES:(j + 1) * LANES]

    @pl.when(i >= nu_ref[0])
    def _():
        y_ref[...] = jnp.zeros_like(y_ref)


def _experts(be, nused, xs, w1, b1, w2, b2, layer):
    ns = xs.shape[0] // TOK_TILE
    depth, e, d, f2 = w1.shape
    f = w2.shape[2]
    nblk = ns // MOE_BLOCK
    used = lambda i, nu: jnp.minimum(i, nu[0] - 1)
    grid_spec = pltpu.PrefetchScalarGridSpec(
        num_scalar_prefetch=2,
        grid=(nblk,),
        in_specs=[
            pl.BlockSpec((MOE_BLOCK * TOK_TILE, LANES), lambda i, be, nu: (used(i, nu), 0)),
            pl.BlockSpec((1, 1, d, f2), lambda i, be, nu: (layer, be[i], 0, 0)),
            pl.BlockSpec((1, 1, 1, f2), lambda i, be, nu: (layer, be[i], 0, 0)),
            pl.BlockSpec((1, 1, f, d), lambda i, be, nu: (layer, be[i], 0, 0)),
            pl.BlockSpec((1, 1, 1, d), lambda i, be, nu: (layer, be[i], 0, 0)),
        ],
        out_specs=pl.BlockSpec((MOE_BLOCK * TOK_TILE, LANES), lambda i, be, nu: (i, 0)),
        scratch_shapes=[pltpu.VMEM((d, f2), BF16), pltpu.VMEM((f, d), BF16)],
    )
    return pl.pallas_call(
        _expert_body,
        grid_spec=grid_spec,
        out_shape=jax.ShapeDtypeStruct((ns * TOK_TILE, LANES), F32),
        compiler_params=_params(("arbitrary",)),
        name="moe_experts",
    )(be, nused, xs, w1, b1.reshape(depth, e, 1, f2), w2, b2.reshape(depth, e, 1, d))


def _combine_body(slot_ref, y_ref, x1_ref, gt_ref, gf_ref, fw_ref, o_ref, buf, sem, *, final):
    td = x1_ref.shape[0]

    def issue(g, c):
        for u in range(DMA_UNROLL):
            tok = g * DMA_UNROLL + u
            for kk in range(TOP_K):
                pltpu.make_async_copy(y_ref.at[_tile(slot_ref[kk, tok])], buf.at[_tile(kk * td + tok)], sem).start()
        return c

    lax.fori_loop(0, td // DMA_UNROLL, issue, 0)
    for kk in range(TOP_K):
        pltpu.make_async_copy(y_ref.at[pl.ds(0, td * TOK_TILE)], buf.at[pl.ds(kk * td * TOK_TILE, td * TOK_TILE)], sem).wait()
    g = gt_ref[...]
    gk = [g[:, kk:kk + 1] for kk in range(TOP_K)]
    parts = []
    for j in range(TOK_TILE):
        y = gk[0] * buf[pl.ds(j, td, stride=TOK_TILE), :]
        for kk in range(1, TOP_K):
            y = y + gk[kk] * buf[pl.ds(kk * td * TOK_TILE + j, td, stride=TOK_TILE), :]
        parts.append(x1_ref[:, j * LANES:(j + 1) * LANES] + gf_ref[0][:, j * LANES:(j + 1) * LANES] * y)
    x2 = jnp.concatenate(parts, axis=-1)
    if final:
        ms = jnp.mean(x2 * x2, axis=-1, keepdims=True)
        x2 = x2 * lax.rsqrt(ms + NORM_EPS) * fw_ref[...]
    o_ref[...] = x2


def _combine(slot, ys, x1, gt, gate_f, fw, seq_len, final):
    t, d = x1.shape
    td = 256
    per_b = seq_len // td
    return pl.pallas_call(
        functools.partial(_combine_body, final=final),
        grid=(t // td,),
        in_specs=[pl.BlockSpec((2 * TOP_K, td), lambda i: (0, i), memory_space=pltpu.SMEM),
                  pl.BlockSpec(memory_space=pl.ANY),
                  pl.BlockSpec((td, d), lambda i: (i, 0)),
                  pl.BlockSpec((td, LANES), lambda i: (i, 0)),
                  pl.BlockSpec((1, 1, d), lambda i: (i // per_b, 0, 0)),
                  pl.BlockSpec((1, d), lambda i: (0, 0))],
        out_specs=pl.BlockSpec((td, d), lambda i: (i, 0)),
        out_shape=jax.ShapeDtypeStruct((t, d), F32),
        scratch_shapes=[pltpu.VMEM((TOP_K * td * TOK_TILE, LANES), F32), pltpu.SemaphoreType.DMA],
        compiler_params=_params(("arbitrary",)),
        name="moe_combine",
    )(slot, ys, x1, gt, gate_f, fw)


def _split_w_in(w):
    hw = HGRN_HEADS * HGRN_DK
    aw = ATT_HEADS * ATT_DH
    iw = IDX_HEADS * IDX_DH
    d = w.shape[0]
    o = 4 * hw
    qb = w[:, o:o + aw]
    kb = w[:, o + aw:o + aw + ATT_DH]
    vb = w[:, o + aw + ATT_DH:o + aw + 2 * ATT_DH]
    o2 = o + aw + 2 * ATT_DH
    iq = w[:, o2:o2 + iw]
    ik = w[:, o2 + iw:o2 + iw + IDX_DH]
    ih = w[:, o2 + iw + IDX_DH:o2 + iw + IDX_DH + IDX_HEADS]
    o3 = o2 + iw + IDX_DH + IDX_HEADS
    small = jnp.concatenate([kb, vb, ik, ih, jnp.zeros((d, 2 * LANES - 3 * ATT_DH - IDX_HEADS), w.dtype)], axis=1)
    groups = (w[:, :o], jnp.concatenate([qb, iq], axis=1), small, w[:, o3:])
    return tuple(g.astype(BF16) for g in groups)


def kernel(x, c, w_ada, b_ada, norm_mix_w, w_in, hgrn_lb_logits, hgrn_norm_w, rel_bias, w_proj_a, w_proj_b,
           w_out, norm_ffn_w, w_router, b_router, w_mlp1, b_mlp1, w_mlp2, b_mlp2, final_norm_w):
    bsz, seq_len, d = x.shape
    depth = w_in.shape[0]
    n_exp = w_router.shape[2]
    t = bsz * seq_len
    n_slots = t * TOP_K + n_exp * MOE_BLOCK
    nblk = n_slots // MOE_BLOCK
    nbp = -(-nblk // LANES) * LANES

    mod = _ada(c, w_ada, b_ada)
    bias_tiles = _bias_tiles(rel_bias)
    x2d = x.reshape(t, d)
    for l in range(depth):
        m6 = mod[l].reshape(bsz, N_MOD, 1, d)
        shift_m, scale_m, gate_m, shift_f, scale_f, gate_f = (m6[:, n] for n in range(N_MOD))
        zh, zq, zs, zg = _inproj(x2d, shift_m, scale_m, norm_mix_w[l].reshape(1, d), _split_w_in(w_in[l]), seq_len)
        oa = _hgrn(zh, hgrn_lb_logits, hgrn_norm_w[l], l, bsz, seq_len).reshape(t, -1)
        ob = _dsa(zq, zs, bias_tiles, bsz, seq_len)
        x1, h2, lgt = _merge(x2d, oa, ob, zg, gate_m, shift_f, scale_f, norm_ffn_w[l].reshape(1, d),
                             w_proj_a[l].astype(BF16), w_proj_b[l].astype(BF16), w_out[l].astype(BF16),
                             w_router[l].T, b_router[l].reshape(n_exp, 1), seq_len)
        er, gt, cnt = _route1(lgt)
        slot, be, tot, fill_start = _route2(er, cnt, nbp)
        nused = (tot[0, :1] >> int(math.log2(MOE_BLOCK))).astype(I32)
        xs = _dispatch(fill_start[:, 0], nused, slot, h2, n_slots)
        ys = _experts(be[0, :nblk], nused, xs, w_mlp1, b_mlp1, w_mlp2, b_mlp2, l)
        x2d = _combine(slot, ys, x1, gt, gate_f, final_norm_w.reshape(1, d), seq_len, final=(l == depth - 1))
    return x2d.reshape(bsz, seq_len, d)
```

```python
import functools
import math

import jax
import jax.numpy as jnp
from jax import lax
from jax.experimental import pallas as pl
from jax.experimental.pallas import tpu as pltpu

F32 = jnp.float32
BF16 = jnp.bfloat16
I32 = jnp.int32
I16 = jnp.int16

HGRN_HEADS = 4
HGRN_DK = 128
ATT_HEADS = 8
ATT_DH = 64
IDX_HEADS = 8
IDX_DH = 64
TOPK_MAX = 256
REL_BUCKETS = 32
REL_MAX_DIST = 128
TOP_K = 4
SWIGLU_ALPHA = 1.702
SWIGLU_LIMIT = 7.0
NORM_EPS = 1e-6
N_MOD = 6

LANES = 128
HGRN_CHUNK = 128
HGRN_SUB = 4
KEY_BLOCK = 128
Q_ROWS = 256
MOE_BLOCK = 256
DMA_UNROLL = 8
TOK_TILE = 8
VMEM_LIMIT = 56 * 1024 * 1024
NEG_INF = float("-inf")
INT_MIN = -(2 ** 31)


def _dot(a, b):
    return jnp.dot(a, b, preferred_element_type=F32)


def _dot_nt(a, b):
    return lax.dot_general(a, b, (((1,), (1,)), ((), ())), preferred_element_type=F32)


def _dot_tn(a, b):
    return lax.dot_general(a, b, (((0,), (0,)), ((), ())), preferred_element_type=F32)


def _split2(a):
    hi = a.astype(BF16)
    lo = (a - hi.astype(F32)).astype(BF16)
    return hi, lo


def _split3(a):
    hi = a.astype(BF16)
    r = a - hi.astype(F32)
    mid = r.astype(BF16)
    lo = (r - mid.astype(F32)).astype(BF16)
    return hi, mid, lo


def _sigmoid(x):
    return 0.5 * jnp.tanh(0.5 * x) + 0.5


def _params(sem):
    return pltpu.CompilerParams(dimension_semantics=sem, vmem_limit_bytes=VMEM_LIMIT)


def _ada_body(c_ref, w_ref, b_ref, o_ref):
    c = c_ref[...]
    ca = c * _sigmoid(c)
    o_ref[0] = _dot(ca, w_ref[0]) + b_ref[0]


def _ada(c, w_ada, b_ada):
    depth, d, nd = w_ada.shape
    bsz = c.shape[0]
    tn = 1024
    return pl.pallas_call(
        _ada_body,
        grid=(depth, nd // tn),
        in_specs=[
            pl.BlockSpec((bsz, d), lambda l, j: (0, 0)),
            pl.BlockSpec((1, d, tn), lambda l, j: (l, 0, j)),
            pl.BlockSpec((1, 1, tn), lambda l, j: (l, 0, j)),
        ],
        out_specs=pl.BlockSpec((1, bsz, tn), lambda l, j: (l, 0, j)),
        out_shape=jax.ShapeDtypeStruct((depth, bsz, nd), F32),
        compiler_params=_params(("parallel", "parallel")),
        name="ada_mod",
    )(c, w_ada, b_ada.reshape(depth, 1, nd))


def _rms_mod(x, nw, scale, shift):
    ms = jnp.mean(x * x, axis=-1, keepdims=True)
    y = x * lax.rsqrt(ms + NORM_EPS) * nw
    return y * (1.0 + scale) + shift


def _inproj_body(x_ref, sh_ref, sc_ref, nw_ref, w1, w2, w3, w4, zh, zq, zs, zg):
    h = _rms_mod(x_ref[...], nw_ref[...], sc_ref[0], sh_ref[0]).astype(BF16)
    zh[...] = _dot(h, w1[...])
    zq[...] = _dot(h, w2[...]).astype(BF16)
    zs[...] = _dot(h, w3[...])
    zg[...] = _dot(h, w4[...])


def _inproj(x2d, shift, scale, nw, ws, seq_len):
    t, d = x2d.shape
    tm = 256
    per_b = seq_len // tm
    w1, w2, w3, w4 = ws
    row = lambda i: (i, 0)
    const = lambda i: (0, 0)
    bmap = lambda i: (i // per_b, 0, 0)
    outs = [
        jax.ShapeDtypeStruct((t, w1.shape[1]), F32),
        jax.ShapeDtypeStruct((t, w2.shape[1]), BF16),
        jax.ShapeDtypeStruct((t, w3.shape[1]), F32),
        jax.ShapeDtypeStruct((t, w4.shape[1]), F32),
    ]
    return pl.pallas_call(
        _inproj_body,
        grid=(t // tm,),
        in_specs=[
            pl.BlockSpec((tm, d), row),
            pl.BlockSpec((1, 1, d), bmap),
            pl.BlockSpec((1, 1, d), bmap),
            pl.BlockSpec((1, d), const),
            pl.BlockSpec(w1.shape, const),
            pl.BlockSpec(w2.shape, const),
            pl.BlockSpec(w3.shape, const),
            pl.BlockSpec(w4.shape, const),
        ],
        out_specs=[pl.BlockSpec((tm, o.shape[1]), row) for o in outs],
        out_shape=outs,
        compiler_params=_params(("parallel",)),
        name="in_proj",
    )(x2d, shift, scale, nw, w1, w2, w3, w4)


def _hgrn_body(lbl_ref, qa_ref, fa_ref, ia_ref, ga_ref, nw_ref, o_ref, *, layer, seq_len):
    C, R = HGRN_CHUNK, HGRN_SUB
    lg = lbl_ref[...]
    ex = jnp.exp(lg - jnp.max(lg, axis=0, keepdims=True))
    sm = ex / jnp.sum(ex, axis=0, keepdims=True)
    lb = jnp.zeros((1, LANES), F32)
    for li in range(1, layer + 1):
        lb = lb + sm[li]
    log_lb = jnp.log(lb)
    log_1m = jnp.log1p(-lb)
    nw = nw_ref[...]

    rowi = lax.broadcasted_iota(I32, (C, LANES), 0)
    ri2 = lax.broadcasted_iota(I32, (C, C), 0)
    ci2 = lax.broadcasted_iota(I32, (C, C), 1)
    ltri = jnp.where(ci2 <= ri2, 1.0, 0.0).astype(BF16)
    ones_bf = jnp.ones((LANES, LANES), BF16)
    dlt = ri2 - ci2

    def chunk(ci, st):
        r0 = pl.multiple_of(ci * C, C)
        qa = qa_ref[0, pl.ds(r0, C), :]
        z = fa_ref[0, pl.ds(r0, C), :]
        v = ia_ref[0, pl.ds(r0, C), :]
        ga = ga_ref[0, pl.ds(r0, C), :]
        q = qa * _sigmoid(qa)
        ls = jnp.minimum(z, 0.0) - jnp.log1p(jnp.exp(-jnp.abs(z)))
        t2 = log_1m + ls
        mx = jnp.maximum(log_lb, t2)
        g = mx + jnp.log(jnp.exp(log_lb - mx) + jnp.exp(t2 - mx))
        k = (1.0 - lb) * _sigmoid(-z)
        g1, g2, g3 = _split3(g)
        b = _dot(ltri, g1) + (_dot(ltri, g2) + _dot(ltri, g3))

        a = jnp.zeros((C, C), F32)
        half = C // 2
        while half >= R:
            span = 2 * half
            nb = C // span
            refb = jnp.concatenate(
                [jnp.broadcast_to(b[p * span + half - 1:p * span + half, :], (span, LANES)) for p in range(nb)],
                axis=0)
            second = (rowi & (span - 1)) >= half
            qt = q * jnp.exp(jnp.where(second, b - refb, NEG_INF))
            kt = k * jnp.exp(jnp.where(second, NEG_INF, refb - b))
            al = _dot_nt(qt.astype(BF16), kt.astype(BF16))
            sh = int(math.log2(span))
            a = a + jnp.where((ri2 >> sh) == (ci2 >> sh), al, 0.0)
            half //= 2

        ps = []
        for dl in range(R):
            if dl == 0:
                kd, bd = k, b
            else:
                kd = pltpu.roll(k, dl, 0)
                bd = pltpu.roll(b, dl, 0)
            arg = jnp.where((rowi & (R - 1)) >= dl, b - bd, NEG_INF)
            ps.append((q * kd * jnp.exp(arg)).astype(BF16))
        rs = _dot(jnp.concatenate(ps, axis=0), ones_bf)
        ad = jnp.zeros((C, C), F32)
        for dl in range(R):
            ad = jnp.where(dlt == dl, rs[dl * C:(dl + 1) * C, :], ad)
        shr = int(math.log2(R))
        a = a + jnp.where((ri2 >> shr) == (ci2 >> shr), ad, 0.0)

        vb = v.astype(BF16)
        o = _dot(a.astype(BF16), vb) + _dot_nt((q * jnp.exp(b)).astype(BF16), st.astype(BF16))
        bl = b[C - 1:C, :]
        kdec = (k * jnp.exp(bl - b)).astype(BF16)
        st_new = st * jnp.exp(bl) + _dot_tn(vb, kdec)
        on = o * lax.rsqrt(jnp.mean(o * o, axis=-1, keepdims=True) + NORM_EPS) * nw
        o_ref[0, pl.ds(r0, C), :] = (on * (ga * _sigmoid(ga))).astype(o_ref.dtype)
        return st_new

    lax.fori_loop(0, seq_len // C, chunk, jnp.zeros((LANES, LANES), F32))


def _hgrn(zh, lb_logits, norm_w, layer, bsz, seq_len):
    depth = lb_logits.shape[0]
    hw = HGRN_HEADS * HGRN_DK
    z3 = zh.reshape(bsz, seq_len, zh.shape[1])
    blk = lambda off: pl.BlockSpec((1, seq_len, LANES), lambda b, h: (b, 0, off + h))
    return pl.pallas_call(
        functools.partial(_hgrn_body, layer=layer, seq_len=seq_len),
        grid=(bsz, HGRN_HEADS),
        in_specs=[
            pl.BlockSpec((depth, 1, LANES), lambda b, h: (0, 0, h)),
            blk(0), blk(HGRN_HEADS), blk(2 * HGRN_HEADS), blk(3 * HGRN_HEADS),
            pl.BlockSpec((1, LANES), lambda b, h: (0, h)),
        ],
        out_specs=pl.BlockSpec((1, seq_len, LANES), lambda b, h: (b, 0, h)),
        out_shape=jax.ShapeDtypeStruct((bsz, seq_len, hw), BF16),
        compiler_params=_params(("parallel", "parallel")),
        name="hgrn2",
    )(lb_logits.reshape(depth, 1, hw), z3, z3, z3, z3, norm_w.reshape(1, hw))


def _float_key(x):
    x = jnp.where(x == 0.0, 0.0, x)
    bits = lax.bitcast_convert_type(x, I32)
    return bits ^ ((bits >> 31) & 0x7FFFFFFF)


def _dsa_body(zq_ref, zsq_ref, zs_ref, bias_ref, o_ref,
              key_ref, hi_ref, lo_ref, sel_ref, k2_ref, ik2_ref, v2t_ref, acc_ref, *, k_sel):
    i = pl.program_id(1)
    tq, kb = Q_ROWS, KEY_BLOCK
    sub = tq // kb
    nkb = key_ref.shape[0]
    aw = ATT_HEADS * ATT_DH
    npair = ATT_HEADS // 2
    half = LANES // 2
    nb = sub * (i + 1)

    @pl.when(i == 0)
    def _():
        lo = lax.broadcasted_iota(I32, (kb, LANES), 1) < half
        one = jnp.where(lo, 1.0, 0.0)

        def build(j, c):
            r = pl.multiple_of(j * kb, kb)
            kv = zs_ref[pl.ds(r, kb), 0:LANES]
            ix = zs_ref[pl.ds(r, kb), LANES:2 * LANES]
            vk = pltpu.roll(kv, half, 1)
            xi = pltpu.roll(ix, half, 1)
            sc = ATT_DH ** -0.5
            k2_ref[j] = jnp.concatenate([jnp.where(lo, kv * sc, 0.0), jnp.where(lo, 0.0, vk * sc)], axis=0).astype(BF16)
            ik2_ref[j] = jnp.concatenate([jnp.where(lo, ix, 0.0), jnp.where(lo, 0.0, xi)], axis=0).astype(BF16)
            top = jnp.concatenate([jnp.where(lo, vk, 0.0), one], axis=1)
            bot = jnp.concatenate([jnp.where(lo, 0.0, kv), 1.0 - one], axis=1)
            v2t_ref[j] = jnp.concatenate([top, bot], axis=0).T.astype(BF16)
            return c

        lax.fori_loop(0, nkb, build, 0)

    krow = lax.broadcasted_iota(I32, (kb, tq), 0)
    qcol = lax.broadcasted_iota(I32, (kb, tq), 1)
    r1 = lax.broadcasted_iota(I32, (LANES, LANES), 0)
    c1 = lax.broadcasted_iota(I32, (LANES, LANES), 1)
    slt = jnp.where(c1 < r1, 1.0, 0.0).astype(BF16)
    iwt = (zsq_ref[:, LANES:2 * LANES] * (IDX_HEADS ** -0.5 * IDX_DH ** -0.5)).T

    def causal(j):
        return (kb * j + krow) <= (tq * i + qcol)

    def score_blk(j, c):
        acc = jnp.zeros((kb, tq), F32)
        for p in range(IDX_HEADS // 2):
            s = _dot_nt(ik2_ref[j], zq_ref[:, aw + p * LANES:aw + (p + 1) * LANES])
            w0 = iwt[half + 2 * p:half + 2 * p + 1, :]
            w1 = iwt[half + 2 * p + 1:half + 2 * p + 2, :]
            acc = acc + (w0 * jnp.maximum(s[:kb], 0.0) + w1 * jnp.maximum(s[kb:], 0.0))
        key = _float_key(jnp.where(causal(j), acc, NEG_INF))
        key_ref[j] = key
        hi_ref[j] = (key >> 16).astype(I16)
        return c

    lax.fori_loop(0, nb, score_blk, 0)

    kf = float(k_sel)

    def count(pred):
        def blk(j, acc):
            hit = jnp.where(pred(key_ref[j]), 1.0, 0.0)
            return acc + jnp.sum(hit.reshape(kb // 8, 8, tq), axis=0)
        acc = lax.fori_loop(0, nb, blk, jnp.zeros((8, tq), F32))
        return jnp.sum(acc, axis=0, keepdims=True)

    def count16(ref, pred):
        def blk(j, acc):
            hit = jnp.where(pred(ref[j]), jnp.int16(1), jnp.int16(0))
            for r in range(kb // 16):
                acc = acc + hit[16 * r:16 * (r + 1), :]
            return acc
        acc = lax.fori_loop(0, nb, blk, jnp.zeros((16, tq), I16))
        return jnp.sum(acc.astype(F32), axis=0, keepdims=True)

    def kth16(ref, base):
        zero16 = jnp.zeros((1, tq), I16)
        t0 = jnp.where(base + count16(ref, lambda v: v >= zero16) >= kf, 0, -32768).astype(I32)

        def step(it, t):
            cand = t | jnp.left_shift(1, 14 - it)
            c16 = cand.astype(I16)
            return jnp.where(base + count16(ref, lambda v: v >= c16) >= kf, cand, t)

        return lax.fori_loop(0, 15, step, t0)

    t_hi = kth16(hi_ref, 0.0)
    h16 = t_hi.astype(I16)
    above = count16(hi_ref, lambda v: v > h16)

    def low_blk(j, c):
        lo = ((key_ref[j] & 0xFFFF) - 32768).astype(I16)
        lo_ref[j] = jnp.where(hi_ref[j] == h16, lo, jnp.int16(-32768))
        return c

    lax.fori_loop(0, nb, low_blk, 0)
    t_lo = kth16(lo_ref, above)
    thr = (t_hi << 16) | ((t_lo + 32768) & 0xFFFF)
    need = kf - count(lambda kj: kj > thr)

    def sel_blk(j, carry):
        kj = key_ref[j]
        eq = jnp.where(kj == thr, 1.0, 0.0)
        pref = _dot(slt, eq.astype(BF16)) + carry
        tie_ok = jnp.where(kj == thr, jnp.where(pref < need, 1.0, 0.0), 0.0)
        chosen = jnp.where(kj > thr, 1.0, tie_ok)
        chosen = jnp.where(causal(j), chosen, 0.0)
        sel_ref[j] = jnp.where(chosen > 0.0, 0.0, NEG_INF)
        return carry + jnp.sum(eq, axis=0, keepdims=True)

    lax.fori_loop(0, nb, sel_blk, jnp.zeros((1, tq), F32))

    def logits(j, p):
        bias = jnp.concatenate([bias_ref[jnp.clip(sub * i + u - j, 0, 2), p] for u in range(sub)], axis=1)
        sel = sel_ref[j]
        s = _dot_nt(k2_ref[j], zq_ref[:, p * LANES:(p + 1) * LANES])
        return s + bias + jnp.concatenate([sel, sel], axis=0)

    def col_max(x):
        return jnp.max(jnp.max(x.reshape(kb // 8, 8, tq), axis=0), axis=0, keepdims=True)

    acc_ref[...] = jnp.zeros(acc_ref.shape, F32)
    quarter = kb // 2

    def att_blk(j, ms):
        out = []
        for p in range(npair):
            s = logits(j, p)
            pes, als, mn = [], [], []
            for u, sl in enumerate((s[:kb], s[kb:])):
                m_old = ms[2 * p + u]
                m_new = jnp.maximum(m_old, col_max(sl))
                m_safe = jnp.where(m_new == NEG_INF, 0.0, m_new)
                pes.append(jnp.exp(sl - m_safe))
                als.append(jnp.broadcast_to(jnp.exp(m_old - m_safe), (quarter, tq)))
                mn.append(m_new)
            pe = jnp.concatenate(pes, axis=0).astype(BF16)
            alpha = jnp.concatenate(als + als, axis=0)
            acc_ref[p] = acc_ref[p] * alpha + _dot(v2t_ref[j], pe)
            out += mn
        return tuple(out)

    lax.fori_loop(0, nb, att_blk, tuple(jnp.full((1, tq), NEG_INF, F32) for _ in range(ATT_HEADS)))
    for p in range(npair):
        a = acc_ref[p]
        o_ref[:, p * LANES:(p + 1) * LANES] = (a[:kb] / a[kb:]).T.astype(o_ref.dtype)


def _t5_bucket(rel):
    max_exact = REL_BUCKETS // 2
    rel_f = jnp.maximum(rel, 1).astype(F32)
    large = max_exact + (jnp.log(rel_f / max_exact) / math.log(REL_MAX_DIST / max_exact)
                         * (REL_BUCKETS - max_exact)).astype(I32)
    large = jnp.minimum(large, REL_BUCKETS - 1)
    return jnp.where(rel < max_exact, rel, large)


def _bias_tiles(rel_bias):
    kb = KEY_BLOCK
    assert kb >= REL_MAX_DIST
    tq = jnp.arange(kb)[:, None]
    sk = jnp.arange(kb)[None, :]
    tiles = []
    for off in range(3):
        rel = jnp.maximum(off * kb + tq - sk, 0)
        bucket = _t5_bucket(rel)[..., None]
        tile = jnp.zeros((kb, kb, rel_bias.shape[1]), F32)
        for b in range(REL_BUCKETS):
            tile = jnp.where(bucket == b, rel_bias[b].astype(F32), tile)
        per_head = jnp.transpose(tile, (2, 1, 0))
        tiles.append(jnp.concatenate([per_head[0::2], per_head[1::2]], axis=-2))
    return jnp.stack(tiles, axis=0).astype(F32)


def _dsa(zq, zs, bias_tiles, bsz, seq_len):
    t = zq.shape[0]
    nq = seq_len // Q_ROWS
    nkb = seq_len // KEY_BLOCK
    k_sel = min(TOPK_MAX, seq_len // 4)
    aw = ATT_HEADS * ATT_DH
    npair = ATT_HEADS // 2
    return pl.pallas_call(
        functools.partial(_dsa_body, k_sel=k_sel),
        grid=(bsz, nq),
        in_specs=[
            pl.BlockSpec((Q_ROWS, zq.shape[1]), lambda b, i: (b * nq + i, 0)),
            pl.BlockSpec((Q_ROWS, zs.shape[1]), lambda b, i: (b * nq + i, 0)),
            pl.BlockSpec((seq_len, zs.shape[1]), lambda b, i: (b, 0)),
            pl.BlockSpec(bias_tiles.shape, lambda b, i: (0, 0, 0, 0)),
        ],
        out_specs=pl.BlockSpec((Q_ROWS, aw), lambda b, i: (b * nq + i, 0)),
        out_shape=jax.ShapeDtypeStruct((t, aw), BF16),
        scratch_shapes=[
            pltpu.VMEM((nkb, KEY_BLOCK, Q_ROWS), I32),
            pltpu.VMEM((nkb, KEY_BLOCK, Q_ROWS), I16),
            pltpu.VMEM((nkb, KEY_BLOCK, Q_ROWS), I16),
            pltpu.VMEM((nkb, KEY_BLOCK, Q_ROWS), F32),
            pltpu.VMEM((nkb, 2 * KEY_BLOCK, LANES), BF16),
            pltpu.VMEM((nkb, 2 * KEY_BLOCK, LANES), BF16),
            pltpu.VMEM((nkb, 2 * LANES, 2 * KEY_BLOCK), BF16),
            pltpu.VMEM((npair, 2 * LANES, Q_ROWS), F32),
        ],
        compiler_params=_params(("arbitrary", "arbitrary")),
        name="dsa",
    )(zq, zs, zs, bias_tiles)


def _merge_body(x_ref, oa_ref, ob_ref, zg_ref, gm_ref, shf_ref, scf_ref, nw_ref,
                wpa, wpb, wout, wrt, br, x1_ref, h2_ref, lgt_ref):
    d = x_ref.shape[1]
    ya = _dot(oa_ref[...], wpa[...])
    yb = _dot(ob_ref[...], wpb[...])
    merged = _sigmoid(zg_ref[:, :d]) * ya + _sigmoid(zg_ref[:, d:]) * yb
    y = _dot(merged.astype(BF16), wout[...])
    x1 = x_ref[...] + gm_ref[0] * y
    x1_ref[...] = x1
    h2 = _rms_mod(x1, nw_ref[...], scf_ref[0], shf_ref[0])
    for j in range(TOK_TILE):
        h2_ref[pl.ds(j, x_ref.shape[0], stride=TOK_TILE), :] = h2[:, j * LANES:(j + 1) * LANES]
    hh, hl = _split2(h2)
    wh, wl = _split2(wrt[...])
    lgt_ref[...] = _dot_nt(wh, hh) + (_dot_nt(wh, hl) + _dot_nt(wl, hh)) + br[...]


def _merge(x2d, oa, ob, zg, gate_m, shift_f, scale_f, nw, wpa, wpb, wout, wrt, br, seq_len):
    t, d = x2d.shape
    assert d == TOK_TILE * LANES
    tm = 256
    per_b = seq_len // tm
    e = wrt.shape[0]
    row = lambda i: (i, 0)
    const = lambda i: (0, 0)
    bmap = lambda i: (i // per_b, 0, 0)
    return pl.pallas_call(
        _merge_body,
        grid=(t // tm,),
        in_specs=[
            pl.BlockSpec((tm, d), row),
            pl.BlockSpec((tm, oa.shape[1]), row),
            pl.BlockSpec((tm, ob.shape[1]), row),
            pl.BlockSpec((tm, zg.shape[1]), row),
            pl.BlockSpec((1, 1, d), bmap),
            pl.BlockSpec((1, 1, d), bmap),
            pl.BlockSpec((1, 1, d), bmap),
            pl.BlockSpec((1, d), const),
            pl.BlockSpec(wpa.shape, const),
            pl.BlockSpec(wpb.shape, const),
            pl.BlockSpec(wout.shape, const),
            pl.BlockSpec(wrt.shape, const),
            pl.BlockSpec(br.shape, const),
        ],
        out_specs=[pl.BlockSpec((tm, d), row), pl.BlockSpec((tm * TOK_TILE, LANES), row),
                   pl.BlockSpec((e, tm), lambda i: (0, i))],
        out_shape=[jax.ShapeDtypeStruct((t, d), F32), jax.ShapeDtypeStruct((t * TOK_TILE, LANES), F32),
                   jax.ShapeDtypeStruct((e, t), F32)],
        compiler_params=_params(("parallel",)),
        name="merge",
    )(x2d, oa, ob, zg, gate_m, shift_f, scale_f, nw, wpa, wpb, wout, wrt, br)


def _route1_body(lg_ref, er_ref, gt_ref, cnt_ref, base_ref):
    i = pl.program_id(0)
    e, tr = lg_ref.shape

    @pl.when(i == 0)
    def _():
        base_ref[...] = jnp.zeros_like(base_ref)

    l = lg_ref[...]
    eidx = lax.broadcasted_iota(I32, (e, tr), 0).astype(F32)
    vals, es, ohs = [], [], []
    for _ in range(TOP_K):
        m = jnp.max(l, axis=0, keepdims=True)
        ek = jnp.min(jnp.where(l == m, eidx, float(e)), axis=0, keepdims=True)
        oh = eidx == ek
        l = jnp.where(oh, NEG_INF, l)
        vals.append(m)
        es.append(ek)
        ohs.append(oh)
    ps = [jnp.exp(v - vals[0]) for v in vals]
    den = ps[0] + ps[1] + ps[2] + ps[3]
    gs = [p / den for p in ps]

    member = jnp.zeros((e, tr), F32)
    for oh in ohs:
        member = member + jnp.where(oh, 1.0, 0.0)
    mb = member.astype(BF16)
    r2 = lax.broadcasted_iota(I32, (tr, tr), 0)
    c2 = lax.broadcasted_iota(I32, (tr, tr), 1)
    sut = jnp.where(r2 < c2, 1.0, 0.0).astype(BF16)
    base = base_ref[...]
    pref = _dot(mb, sut) + jnp.concatenate([base] * (tr // LANES), axis=1)
    ranks = [jnp.sum(jnp.where(oh, pref, 0.0), axis=0, keepdims=True) for oh in ohs]
    er_ref[...] = jnp.concatenate(es + ranks, axis=0).astype(I32)

    row8 = lax.broadcasted_iota(I32, (8, tr), 0)
    g8 = jnp.zeros((8, tr), F32)
    for kk in range(TOP_K):
        g8 = jnp.where(row8 == kk, gs[kk], g8)
    gfull = jnp.concatenate([g8, jnp.zeros((LANES - 8, tr), F32)], axis=0)
    gt_ref[...] = gfull.T

    new_base = base + _dot(mb, jnp.ones((tr, LANES), BF16))
    base_ref[...] = new_base
    cnt_ref[...] = new_base


def _route1(lgt):
    e, t = lgt.shape
    tr = 512
    return pl.pallas_call(
        _route1_body,
        grid=(t // tr,),
        in_specs=[pl.BlockSpec((e, tr), lambda i: (0, i))],
        out_specs=[pl.BlockSpec((2 * TOP_K, tr), lambda i: (0, i)),
                   pl.BlockSpec((tr, LANES), lambda i: (i, 0)),
                   pl.BlockSpec((e, LANES), lambda i: (0, 0))],
        out_shape=[jax.ShapeDtypeStruct((2 * TOP_K, t), I32),
                   jax.ShapeDtypeStruct((t, LANES), F32),
                   jax.ShapeDtypeStruct((e, LANES), F32)],
        scratch_shapes=[pltpu.VMEM((e, LANES), F32)],
        compiler_params=_params(("arbitrary",)),
        name="route_topk",
    )(lgt)


def _route2_body(er_ref, cnt_ref, slot_ref, be_ref, tot_ref, fs_ref):
    e = cnt_ref.shape[0]
    tt = er_ref.shape[1]
    nbp = be_ref.shape[1]
    sh = int(math.log2(MOE_BLOCK))
    cnt = cnt_ref[...].astype(I32)
    padded = ((cnt + (MOE_BLOCK - 1)) >> sh) << sh
    rows = []
    acc = jnp.zeros((1, LANES), I32)
    for ei in range(e):
        rows.append(acc)
        acc = acc + padded[ei:ei + 1, :]
    pstart = jnp.concatenate(rows, axis=0)
    pend = pstart + padded
    tot_ref[...] = jnp.broadcast_to(acc, tot_ref.shape)
    fs_ref[...] = jnp.where(padded > 0, pend - MOE_BLOCK, -1)

    er = er_ref[...]
    eidx = lax.broadcasted_iota(I32, (e, tt), 0)
    pst = jnp.concatenate([pstart] * (tt // LANES), axis=1)
    outs = []
    for kk in range(TOP_K):
        ps = jnp.sum(jnp.where(eidx == er[kk:kk + 1, :], pst, 0), axis=0, keepdims=True)
        outs.append(ps + er[TOP_K + kk:TOP_K + kk + 1, :])
    slot_ref[...] = jnp.concatenate(outs + [jnp.zeros((TOP_K, tt), I32)], axis=0)

    blk0 = lax.broadcasted_iota(I32, (e, nbp), 1) * MOE_BLOCK
    pe = jnp.concatenate([pend] * (nbp // LANES), axis=1)
    nle = jnp.sum(jnp.where(pe <= blk0, 1, 0), axis=0, keepdims=True)
    be_ref[...] = jnp.broadcast_to(jnp.minimum(nle, e - 1), be_ref.shape)


def _route2(er, cnt, nbp):
    t = er.shape[1]
    tt = min(2048, t)
    e = cnt.shape[0]
    return pl.pallas_call(
        _route2_body,
        grid=(t // tt,),
        in_specs=[pl.BlockSpec((2 * TOP_K, tt), lambda i: (0, i)),
                  pl.BlockSpec((e, LANES), lambda i: (0, 0))],
        out_specs=[pl.BlockSpec((2 * TOP_K, tt), lambda i: (0, i)),
                   pl.BlockSpec((8, nbp), lambda i: (0, 0)),
                   pl.BlockSpec((8, LANES), lambda i: (0, 0)),
                   pl.BlockSpec((e, LANES), lambda i: (0, 0))],
        out_shape=[jax.ShapeDtypeStruct((2 * TOP_K, t), I32),
                   jax.ShapeDtypeStruct((8, nbp), I32),
                   jax.ShapeDtypeStruct((8, LANES), I32),
                   jax.ShapeDtypeStruct((e, LANES), I32)],
        compiler_params=_params(("arbitrary",)),
        name="route_slots",
    )(er, cnt)


def _tile(row):
    return pl.ds(pl.multiple_of(row * TOK_TILE, TOK_TILE), TOK_TILE)


def _dispatch_body(fs_ref, nu_ref, slot_ref, h_ref, xs_ref, zbuf, sem, zsem):
    i = pl.program_id(0)
    td = slot_ref.shape[1]

    @pl.when(i == 0)
    def _():
        zbuf[...] = jnp.zeros_like(zbuf)
        zrows = zbuf.shape[0]
        nblk = xs_ref.shape[0] // zrows
        fills = []
        for e in range(fs_ref.shape[0]):
            fills.append((fs_ref[e] >= 0, jnp.maximum(fs_ref[e], 0) * TOK_TILE))
        for b in range(fs_ref.shape[0]):
            blk = nu_ref[0] + b
            fills.append((blk < nblk, jnp.minimum(blk, nblk - 1) * zrows))
        for pred, start in fills:
            @pl.when(pred)
            def _(start=start):
                pltpu.make_async_copy(zbuf, xs_ref.at[pl.ds(pl.multiple_of(start, zrows), zrows)], zsem).start()
        for pred, _ in fills:
            @pl.when(pred)
            def _():
                pltpu.make_async_copy(zbuf, xs_ref.at[pl.ds(0, zrows)], zsem).wait()

    def issue(g, c):
        for u in range(DMA_UNROLL):
            tok = g * DMA_UNROLL + u
            for kk in range(TOP_K):
                pltpu.make_async_copy(h_ref.at[_tile(tok)], xs_ref.at[_tile(slot_ref[kk, tok])], sem).start(
                    priority=(u * TOP_K + kk) % 2)
        return c

    lax.fori_loop(0, td // DMA_UNROLL, issue, 0)
    for kk in range(TOP_K):
        pltpu.make_async_copy(h_ref, xs_ref.at[pl.ds(0, td * TOK_TILE)], sem).wait()


def _dispatch(fill_start, nused, slot, h2, n_slots):
    t = h2.shape[0] // TOK_TILE
    td = 512
    grid_spec = pltpu.PrefetchScalarGridSpec(
        num_scalar_prefetch=2,
        grid=(t // td,),
        in_specs=[pl.BlockSpec((2 * TOP_K, td), lambda i, fs, nu: (0, i), memory_space=pltpu.SMEM),
                  pl.BlockSpec((td * TOK_TILE, LANES), lambda i, fs, nu: (i, 0))],
        out_specs=pl.BlockSpec(memory_space=pl.ANY),
        scratch_shapes=[pltpu.VMEM((MOE_BLOCK * TOK_TILE, LANES), h2.dtype), pltpu.SemaphoreType.DMA, pltpu.SemaphoreType.DMA],
    )
    return pl.pallas_call(
        _dispatch_body,
        grid_spec=grid_spec,
        out_shape=jax.ShapeDtypeStruct((n_slots * TOK_TILE, LANES), h2.dtype),
        compiler_params=_params(("arbitrary",)),
        name="moe_dispatch",
    )(fill_start, nused, slot, h2)


def _expert_body(be_ref, nu_ref, xs_ref, w1_ref, b1_ref, w2_ref, b2_ref, y_ref, w1b, w2b):
    i = pl.program_id(0)
    f = w2_ref.shape[2]

    @pl.when(jnp.logical_or(i == 0, be_ref[i] != be_ref[jnp.maximum(i - 1, 0)]))
    def _():
        w1b[...] = w1_ref[0, 0].astype(BF16)
        w2b[...] = w2_ref[0, 0].astype(BF16)

    @pl.when(i < nu_ref[0])
    def _():
        x = jnp.concatenate([xs_ref[pl.ds(j, MOE_BLOCK, stride=TOK_TILE), :] for j in range(TOK_TILE)], axis=1)
        hm = _dot(x.astype(BF16), w1b[...]) + b1_ref[0, 0]
        glu = jnp.minimum(hm[:, :f], SWIGLU_LIMIT)
        lin = jnp.clip(hm[:, f:], -SWIGLU_LIMIT, SWIGLU_LIMIT)
        act = glu * _sigmoid(SWIGLU_ALPHA * glu) * (lin + 1.0)
        y = _dot(act.astype(BF16), w2b[...]) + b2_ref[0, 0]
        for j in range(TOK_TILE):
            y_ref[pl.ds(j, MOE_BLOCK, stride=TOK_TILE), :] = y[:, j * LANES:(j + 1) * LANES]

    @pl.when(i >= nu_ref[0])
    def _():
        y_ref[...] = jnp.zeros_like(y_ref)


def _experts(be, nused, xs, w1, b1, w2, b2, layer):
    ns = xs.shape[0] // TOK_TILE
    depth, e, d, f2 = w1.shape
    f = w2.shape[2]
    nblk = ns // MOE_BLOCK
    used = lambda i, nu: jnp.minimum(i, nu[0] - 1)
    grid_spec = pltpu.PrefetchScalarGridSpec(
        num_scalar_prefetch=2,
        grid=(nblk,),
        in_specs=[
            pl.BlockSpec((MOE_BLOCK * TOK_TILE, LANES), lambda i, be, nu: (used(i, nu), 0)),
            pl.BlockSpec((1, 1, d, f2), lambda i, be, nu: (layer, be[i], 0, 0)),
            pl.BlockSpec((1, 1, 1, f2), lambda i, be, nu: (layer, be[i], 0, 0)),
            pl.BlockSpec((1, 1, f, d), lambda i, be, nu: (layer, be[i], 0, 0)),
            pl.BlockSpec((1, 1, 1, d), lambda i, be, nu: (layer, be[i], 0, 0)),
        ],
        out_specs=pl.BlockSpec((MOE_BLOCK * TOK_TILE, LANES), lambda i, be, nu: (i, 0)),
        scratch_shapes=[pltpu.VMEM((d, f2), BF16), pltpu.VMEM((f, d), BF16)],
    )
    return pl.pallas_call(
        _expert_body,
        grid_spec=grid_spec,
        out_shape=jax.ShapeDtypeStruct((ns * TOK_TILE, LANES), F32),
        compiler_params=_params(("arbitrary",)),
        name="moe_experts",
    )(be, nused, xs, w1, b1.reshape(depth, e, 1, f2), w2, b2.reshape(depth, e, 1, d))


def _combine_body(slot_ref, y_ref, x1_ref, gt_ref, gf_ref, fw_ref, o_ref, buf, sem, *, final):
    td = x1_ref.shape[0]

    def issue(g, c):
        for u in range(DMA_UNROLL):
            tok = g * DMA_UNROLL + u
            for kk in range(TOP_K):
                pltpu.make_async_copy(y_ref.at[_tile(slot_ref[kk, tok])], buf.at[_tile(kk * td + tok)], sem).start(
                    priority=(u * TOP_K + kk) % 2)
        return c

    lax.fori_loop(0, td // DMA_UNROLL, issue, 0)
    for kk in range(TOP_K):
        pltpu.make_async_copy(y_ref.at[pl.ds(0, td * TOK_TILE)], buf.at[pl.ds(kk * td * TOK_TILE, td * TOK_TILE)], sem).wait()
    g = gt_ref[...]
    gk = [g[:, kk:kk + 1] for kk in range(TOP_K)]
    parts = []
    for j in range(TOK_TILE):
        y = gk[0] * buf[pl.ds(j, td, stride=TOK_TILE), :]
        for kk in range(1, TOP_K):
            y = y + gk[kk] * buf[pl.ds(kk * td * TOK_TILE + j, td, stride=TOK_TILE), :]
        parts.append(x1_ref[:, j * LANES:(j + 1) * LANES] + gf_ref[0][:, j * LANES:(j + 1) * LANES] * y)
    x2 = jnp.concatenate(parts, axis=-1)
    if final:
        ms = jnp.mean(x2 * x2, axis=-1, keepdims=True)
        x2 = x2 * lax.rsqrt(ms + NORM_EPS) * fw_ref[...]
    o_ref[...] = x2


def _combine(slot, ys, x1, gt, gate_f, fw, seq_len, final):
    t, d = x1.shape
    td = 256
    per_b = seq_len // td
    return pl.pallas_call(
        functools.partial(_combine_body, final=final),
        grid=(t // td,),
        in_specs=[pl.BlockSpec((2 * TOP_K, td), lambda i: (0, i), memory_space=pltpu.SMEM),
                  pl.BlockSpec(memory_space=pl.ANY),
                  pl.BlockSpec((td, d), lambda i: (i, 0)),
                  pl.BlockSpec((td, LANES), lambda i: (i, 0)),
                  pl.BlockSpec((1, 1, d), lambda i: (i // per_b, 0, 0)),
                  pl.BlockSpec((1, d), lambda i: (0, 0))],
        out_specs=pl.BlockSpec((td, d), lambda i: (i, 0)),
        out_shape=jax.ShapeDtypeStruct((t, d), F32),
        scratch_shapes=[pltpu.VMEM((TOP_K * td * TOK_TILE, LANES), F32), pltpu.SemaphoreType.DMA],
        compiler_params=_params(("arbitrary",)),
        name="moe_combine",
    )(slot, ys, x1, gt, gate_f, fw)


def _split_w_in(w):
    hw = HGRN_HEADS * HGRN_DK
    aw = ATT_HEADS * ATT_DH
    iw = IDX_HEADS * IDX_DH
    d = w.shape[0]
    o = 4 * hw
    qb = w[:, o:o + aw]
    kb = w[:, o + aw:o + aw + ATT_DH]
    vb = w[:, o + aw + ATT_DH:o + aw + 2 * ATT_DH]
    o2 = o + aw + 2 * ATT_DH
    iq = w[:, o2:o2 + iw]
    ik = w[:, o2 + iw:o2 + iw + IDX_DH]
    ih = w[:, o2 + iw + IDX_DH:o2 + iw + IDX_DH + IDX_HEADS]
    o3 = o2 + iw + IDX_DH + IDX_HEADS
    small = jnp.concatenate([kb, vb, ik, ih, jnp.zeros((d, 2 * LANES - 3 * ATT_DH - IDX_HEADS), w.dtype)], axis=1)
    groups = (w[:, :o], jnp.concatenate([qb, iq], axis=1), small, w[:, o3:])
    return tuple(g.astype(BF16) for g in groups)


def kernel(x, c, w_ada, b_ada, norm_mix_w, w_in, hgrn_lb_logits, hgrn_norm_w, rel_bias, w_proj_a, w_proj_b,
           w_out, norm_ffn_w, w_router, b_router, w_mlp1, b_mlp1, w_mlp2, b_mlp2, final_norm_w):
    bsz, seq_len, d = x.shape
    depth = w_in.shape[0]
    n_exp = w_router.shape[2]
    t = bsz * seq_len
    n_slots = t * TOP_K + n_exp * MOE_BLOCK
    nblk = n_slots // MOE_BLOCK
    nbp = -(-nblk // LANES) * LANES

    mod = _ada(c, w_ada, b_ada)
    bias_tiles = _bias_tiles(rel_bias)
    x2d = x.reshape(t, d)
    for l in range(depth):
        m6 = mod[l].reshape(bsz, N_MOD, 1, d)
        shift_m, scale_m, gate_m, shift_f, scale_f, gate_f = (m6[:, n] for n in range(N_MOD))
        zh, zq, zs, zg = _inproj(x2d, shift_m, scale_m, norm_mix_w[l].reshape(1, d), _split_w_in(w_in[l]), seq_len)
        oa = _hgrn(zh, hgrn_lb_logits, hgrn_norm_w[l], l, bsz, seq_len).reshape(t, -1)
        ob = _dsa(zq, zs, bias_tiles, bsz, seq_len)
        x1, h2, lgt = _merge(x2d, oa, ob, zg, gate_m, shift_f, scale_f, norm_ffn_w[l].reshape(1, d),
                             w_proj_a[l].astype(BF16), w_proj_b[l].astype(BF16), w_out[l].astype(BF16),
                             w_router[l].T, b_router[l].reshape(n_exp, 1), seq_len)
        er, gt, cnt = _route1(lgt)
        slot, be, tot, fill_start = _route2(er, cnt, nbp)
        nused = (tot[0, :1] >> int(math.log2(MOE_BLOCK))).astype(I32)
        xs = _dispatch(fill_start[:, 0], nused, slot, h2, n_slots)
        ys = _experts(be[0, :nblk], nused, xs, w_mlp1, b_mlp1, w_mlp2, b_mlp2, l)
        x2d = _combine(slot, ys, x1, gt, gate_f, final_norm_w.reshape(1, d), seq_len, final=(l == depth - 1))
    return x2d.reshape(bsz, seq_len, d)
```

```python
import functools
import math

import jax
import jax.numpy as jnp
from jax import lax
from jax.experimental import pallas as pl
from jax.experimental.pallas import tpu as pltpu

F32 = jnp.float32
BF16 = jnp.bfloat16
I32 = jnp.int32
I16 = jnp.int16

HGRN_HEADS = 4
HGRN_DK = 128
ATT_HEADS = 8
ATT_DH = 64
IDX_HEADS = 8
IDX_DH = 64
TOPK_MAX = 256
REL_BUCKETS = 32
REL_MAX_DIST = 128
TOP_K = 4
SWIGLU_ALPHA = 1.702
SWIGLU_LIMIT = 7.0
NORM_EPS = 1e-6
N_MOD = 6

LANES = 128
HGRN_CHUNK = 128
HGRN_SUB = 4
HGRN_PAIR = 4
KEY_BLOCK = 128
Q_ROWS = 256
MOE_BLOCK = 256
DMA_UNROLL = 8
TOK_TILE = 8
VMEM_LIMIT = 56 * 1024 * 1024
NEG_INF = float("-inf")
INT_MIN = -(2 ** 31)


def _dot(a, b):
    return jnp.dot(a, b, preferred_element_type=F32)


def _dot_nt(a, b):
    return lax.dot_general(a, b, (((1,), (1,)), ((), ())), preferred_element_type=F32)


def _dot_tn(a, b):
    return lax.dot_general(a, b, (((0,), (0,)), ((), ())), preferred_element_type=F32)


def _split2(a):
    hi = a.astype(BF16)
    lo = (a - hi.astype(F32)).astype(BF16)
    return hi, lo


def _split3(a):
    hi = a.astype(BF16)
    r = a - hi.astype(F32)
    mid = r.astype(BF16)
    lo = (r - mid.astype(F32)).astype(BF16)
    return hi, mid, lo


def _sigmoid(x):
    return 0.5 * jnp.tanh(0.5 * x) + 0.5


def _params(sem):
    return pltpu.CompilerParams(dimension_semantics=sem, vmem_limit_bytes=VMEM_LIMIT)


def _ada_body(c_ref, w_ref, b_ref, o_ref):
    c = c_ref[...]
    ca = c * _sigmoid(c)
    o_ref[0] = _dot(ca, w_ref[0]) + b_ref[0]


def _ada(c, w_ada, b_ada):
    depth, d, nd = w_ada.shape
    bsz = c.shape[0]
    tn = 1024
    return pl.pallas_call(
        _ada_body,
        grid=(depth, nd // tn),
        in_specs=[
            pl.BlockSpec((bsz, d), lambda l, j: (0, 0)),
            pl.BlockSpec((1, d, tn), lambda l, j: (l, 0, j)),
            pl.BlockSpec((1, 1, tn), lambda l, j: (l, 0, j)),
        ],
        out_specs=pl.BlockSpec((1, bsz, tn), lambda l, j: (l, 0, j)),
        out_shape=jax.ShapeDtypeStruct((depth, bsz, nd), F32),
        compiler_params=_params(("parallel", "parallel")),
        name="ada_mod",
    )(c, w_ada, b_ada.reshape(depth, 1, nd))


def _rms_mod(x, nw, scale, shift):
    ms = jnp.mean(x * x, axis=-1, keepdims=True)
    y = x * lax.rsqrt(ms + NORM_EPS) * nw
    return y * (1.0 + scale) + shift


def _inproj_body(x_ref, sh_ref, sc_ref, nw_ref, w1, w2, w3, w4, zh, zq, zs, zg):
    h = _rms_mod(x_ref[...], nw_ref[...], sc_ref[0], sh_ref[0]).astype(BF16)
    zh[...] = _dot(h, w1[...])
    zq[...] = _dot(h, w2[...]).astype(BF16)
    zs[...] = _dot(h, w3[...])
    zg[...] = _dot(h, w4[...])


def _inproj(x2d, shift, scale, nw, ws, seq_len):
    t, d = x2d.shape
    tm = 256
    per_b = seq_len // tm
    w1, w2, w3, w4 = ws
    row = lambda i: (i, 0)
    const = lambda i: (0, 0)
    bmap = lambda i: (i // per_b, 0, 0)
    outs = [
        jax.ShapeDtypeStruct((t, w1.shape[1]), F32),
        jax.ShapeDtypeStruct((t, w2.shape[1]), BF16),
        jax.ShapeDtypeStruct((t, w3.shape[1]), F32),
        jax.ShapeDtypeStruct((t, w4.shape[1]), F32),
    ]
    return pl.pallas_call(
        _inproj_body,
        grid=(t // tm,),
        in_specs=[
            pl.BlockSpec((tm, d), row),
            pl.BlockSpec((1, 1, d), bmap),
            pl.BlockSpec((1, 1, d), bmap),
            pl.BlockSpec((1, d), const),
            pl.BlockSpec(w1.shape, const),
            pl.BlockSpec(w2.shape, const),
            pl.BlockSpec(w3.shape, const),
            pl.BlockSpec(w4.shape, const),
        ],
        out_specs=[pl.BlockSpec((tm, o.shape[1]), row) for o in outs],
        out_shape=outs,
        compiler_params=_params(("parallel",)),
        name="in_proj",
    )(x2d, shift, scale, nw, w1, w2, w3, w4)


def _hgrn_body(lbl_ref, qa_ref, fa_ref, ia_ref, ga_ref, nw_ref, o_ref, *, layer, seq_len):
    C, R = HGRN_CHUNK, HGRN_SUB
    nh = qa_ref.shape[2] // LANES
    lg = lbl_ref[...]
    ex = jnp.exp(lg - jnp.max(lg, axis=0, keepdims=True))
    sm = ex / jnp.sum(ex, axis=0, keepdims=True)
    lb_all = jnp.zeros((1, nh * LANES), F32)
    for li in range(1, layer + 1):
        lb_all = lb_all + sm[li]
    nw_all = nw_ref[...]

    rowi = lax.broadcasted_iota(I32, (C, LANES), 0)
    ri2 = lax.broadcasted_iota(I32, (C, C), 0)
    ci2 = lax.broadcasted_iota(I32, (C, C), 1)
    ltri = jnp.where(ci2 <= ri2, 1.0, 0.0).astype(BF16)
    ones_bf = jnp.ones((LANES, LANES), BF16)
    dlt = ri2 - ci2

    def one_head(qa, z, v, ga, st, lb, nw):
        log_lb = jnp.log(lb)
        log_1m = jnp.log1p(-lb)
        q = qa * _sigmoid(qa)
        ls = jnp.minimum(z, 0.0) - jnp.log1p(jnp.exp(-jnp.abs(z)))
        t2 = log_1m + ls
        mx = jnp.maximum(log_lb, t2)
        g = mx + jnp.log(jnp.exp(log_lb - mx) + jnp.exp(t2 - mx))
        k = (1.0 - lb) * _sigmoid(-z)
        g1, g2, g3 = _split3(g)
        b = _dot(ltri, g1) + (_dot(ltri, g2) + _dot(ltri, g3))

        a = jnp.zeros((C, C), F32)
        half = C // 2
        while half >= R:
            span = 2 * half
            nb = C // span
            refb = jnp.concatenate(
                [jnp.broadcast_to(b[p * span + half - 1:p * span + half, :], (span, LANES)) for p in range(nb)],
                axis=0)
            second = (rowi & (span - 1)) >= half
            qt = q * jnp.exp(jnp.where(second, b - refb, NEG_INF))
            kt = k * jnp.exp(jnp.where(second, NEG_INF, refb - b))
            al = _dot_nt(qt.astype(BF16), kt.astype(BF16))
            sh = int(math.log2(span))
            a = a + jnp.where((ri2 >> sh) == (ci2 >> sh), al, 0.0)
            half //= 2

        ps = []
        for dl in range(R):
            if dl == 0:
                kd, bd = k, b
            else:
                kd = pltpu.roll(k, dl, 0)
                bd = pltpu.roll(b, dl, 0)
            arg = jnp.where((rowi & (R - 1)) >= dl, b - bd, NEG_INF)
            ps.append((q * kd * jnp.exp(arg)).astype(BF16))
        rs = _dot(jnp.concatenate(ps, axis=0), ones_bf)
        ad = jnp.zeros((C, C), F32)
        for dl in range(R):
            ad = jnp.where(dlt == dl, rs[dl * C:(dl + 1) * C, :], ad)
        shr = int(math.log2(R))
        a = a + jnp.where((ri2 >> shr) == (ci2 >> shr), ad, 0.0)

        vb = v.astype(BF16)
        o = _dot(a.astype(BF16), vb) + _dot_nt((q * jnp.exp(b)).astype(BF16), st.astype(BF16))
        bl = b[C - 1:C, :]
        kdec = (k * jnp.exp(bl - b)).astype(BF16)
        st_new = st * jnp.exp(bl) + _dot_tn(vb, kdec)
        on = o * lax.rsqrt(jnp.mean(o * o, axis=-1, keepdims=True) + NORM_EPS) * nw
        return (on * (ga * _sigmoid(ga))).astype(o_ref.dtype), st_new

    def chunk(ci, sts):
        rows = pl.ds(pl.multiple_of(ci * C, C), C)
        new = []
        for hd in range(nh):
            ln = slice(hd * LANES, (hd + 1) * LANES)
            out, st_new = one_head(qa_ref[0, rows, ln], fa_ref[0, rows, ln], ia_ref[0, rows, ln], ga_ref[0, rows, ln],
                                   sts[hd], lb_all[:, ln], nw_all[:, ln])
            o_ref[0, rows, ln] = out
            new.append(st_new)
        return tuple(new)

    lax.fori_loop(0, seq_len // C, chunk, tuple(jnp.zeros((LANES, LANES), F32) for _ in range(nh)))


def _hgrn(zh, lb_logits, norm_w, layer, bsz, seq_len):
    depth = lb_logits.shape[0]
    hw = HGRN_HEADS * HGRN_DK
    wb = HGRN_PAIR * LANES
    ng = HGRN_HEADS // HGRN_PAIR
    z3 = zh.reshape(bsz, seq_len, zh.shape[1])
    blk = lambda off: pl.BlockSpec((1, seq_len, wb), lambda b, h: (b, 0, off + h))
    return pl.pallas_call(
        functools.partial(_hgrn_body, layer=layer, seq_len=seq_len),
        grid=(bsz, ng),
        in_specs=[
            pl.BlockSpec((depth, 1, wb), lambda b, h: (0, 0, h)),
            blk(0), blk(ng), blk(2 * ng), blk(3 * ng),
            pl.BlockSpec((1, wb), lambda b, h: (0, h)),
        ],
        out_specs=pl.BlockSpec((1, seq_len, wb), lambda b, h: (b, 0, h)),
        out_shape=jax.ShapeDtypeStruct((bsz, seq_len, hw), BF16),
        compiler_params=_params(("parallel", "parallel")),
        name="hgrn2",
    )(lb_logits.reshape(depth, 1, hw), z3, z3, z3, z3, norm_w.reshape(1, hw))


def _float_key(x):
    x = jnp.where(x == 0.0, 0.0, x)
    bits = lax.bitcast_convert_type(x, I32)
    return bits ^ ((bits >> 31) & 0x7FFFFFFF)


def _dsa_body(zq_ref, zsq_ref, zs_ref, bias_ref, o_ref,
              key_ref, hi_ref, lo_ref, sel_ref, k2_ref, ik2_ref, v2t_ref, acc_ref, *, k_sel):
    i = pl.program_id(1)
    tq, kb = Q_ROWS, KEY_BLOCK
    sub = tq // kb
    nkb = key_ref.shape[0]
    aw = ATT_HEADS * ATT_DH
    npair = ATT_HEADS // 2
    half = LANES // 2
    nb = sub * (i + 1)

    @pl.when(i == 0)
    def _():
        lo = lax.broadcasted_iota(I32, (kb, LANES), 1) < half
        one = jnp.where(lo, 1.0, 0.0)

        def build(j, c):
            r = pl.multiple_of(j * kb, kb)
            kv = zs_ref[pl.ds(r, kb), 0:LANES]
            ix = zs_ref[pl.ds(r, kb), LANES:2 * LANES]
            vk = pltpu.roll(kv, half, 1)
            xi = pltpu.roll(ix, half, 1)
            sc = ATT_DH ** -0.5
            k2_ref[j] = jnp.concatenate([jnp.where(lo, kv * sc, 0.0), jnp.where(lo, 0.0, vk * sc)], axis=0).astype(BF16)
            ik2_ref[j] = jnp.concatenate([jnp.where(lo, ix, 0.0), jnp.where(lo, 0.0, xi)], axis=0).astype(BF16)
            top = jnp.concatenate([jnp.where(lo, vk, 0.0), one], axis=1)
            bot = jnp.concatenate([jnp.where(lo, 0.0, kv), 1.0 - one], axis=1)
            v2t_ref[j] = jnp.concatenate([top, bot], axis=0).T.astype(BF16)
            return c

        lax.fori_loop(0, nkb, build, 0)

    krow = lax.broadcasted_iota(I32, (kb, tq), 0)
    qcol = lax.broadcasted_iota(I32, (kb, tq), 1)
    r1 = lax.broadcasted_iota(I32, (LANES, LANES), 0)
    c1 = lax.broadcasted_iota(I32, (LANES, LANES), 1)
    slt = jnp.where(c1 < r1, 1.0, 0.0).astype(BF16)
    iwt = (zsq_ref[:, LANES:2 * LANES] * (IDX_HEADS ** -0.5 * IDX_DH ** -0.5)).T

    def causal(j):
        return (kb * j + krow) <= (tq * i + qcol)

    def score_blk(j, c):
        acc = jnp.zeros((kb, tq), F32)
        for p in range(IDX_HEADS // 2):
            s = _dot_nt(ik2_ref[j], zq_ref[:, aw + p * LANES:aw + (p + 1) * LANES])
            w0 = iwt[half + 2 * p:half + 2 * p + 1, :]
            w1 = iwt[half + 2 * p + 1:half + 2 * p + 2, :]
            acc = acc + (w0 * jnp.maximum(s[:kb], 0.0) + w1 * jnp.maximum(s[kb:], 0.0))
        key = _float_key(jnp.where(causal(j), acc, NEG_INF))
        key_ref[j] = key
        hi_ref[j] = (key >> 16).astype(I16)
        return c

    lax.fori_loop(0, nb, score_blk, 0)

    kf = float(k_sel)

    def count(pred):
        def blk(j, acc):
            hit = jnp.where(pred(key_ref[j]), 1.0, 0.0)
            return acc + jnp.sum(hit.reshape(kb // 8, 8, tq), axis=0)
        acc = lax.fori_loop(0, nb, blk, jnp.zeros((8, tq), F32))
        return jnp.sum(acc, axis=0, keepdims=True)

    def count16(ref, pred):
        def blk(j, acc):
            hit = jnp.where(pred(ref[j]), jnp.int16(1), jnp.int16(0))
            for r in range(kb // 16):
                acc = acc + hit[16 * r:16 * (r + 1), :]
            return acc
        acc = lax.fori_loop(0, nb, blk, jnp.zeros((16, tq), I16))
        return jnp.sum(acc.astype(F32), axis=0, keepdims=True)

    def kth16(ref, base):
        zero16 = jnp.zeros((1, tq), I16)
        t0 = jnp.where(base + count16(ref, lambda v: v >= zero16) >= kf, 0, -32768).astype(I32)

        def step(it, t):
            cand = t | jnp.left_shift(1, 14 - it)
            c16 = cand.astype(I16)
            return jnp.where(base + count16(ref, lambda v: v >= c16) >= kf, cand, t)

        return lax.fori_loop(0, 15, step, t0)

    t_hi = kth16(hi_ref, 0.0)
    h16 = t_hi.astype(I16)
    above = count16(hi_ref, lambda v: v > h16)

    def low_blk(j, c):
        lo = ((key_ref[j] & 0xFFFF) - 32768).astype(I16)
        lo_ref[j] = jnp.where(hi_ref[j] == h16, lo, jnp.int16(-32768))
        return c

    lax.fori_loop(0, nb, low_blk, 0)
    t_lo = kth16(lo_ref, above)
    thr = (t_hi << 16) | ((t_lo + 32768) & 0xFFFF)
    need = kf - count(lambda kj: kj > thr)

    excess = jnp.max(count(lambda kj: kj == thr) - need) > 0.5

    @pl.when(excess)
    def _():
        def sel_blk(j, carry):
            kj = key_ref[j]
            eq = jnp.where(kj == thr, 1.0, 0.0)
            pref = _dot(slt, eq.astype(BF16)) + carry
            tie_ok = jnp.where(kj == thr, jnp.where(pref < need, 1.0, 0.0), 0.0)
            chosen = jnp.where(kj > thr, 1.0, tie_ok)
            chosen = jnp.where(causal(j), chosen, 0.0)
            sel_ref[j] = jnp.where(chosen > 0.0, 0.0, NEG_INF)
            return carry + jnp.sum(eq, axis=0, keepdims=True)

        lax.fori_loop(0, nb, sel_blk, jnp.zeros((1, tq), F32))

    @pl.when(jnp.logical_not(excess))
    def _():
        def sel_blk(j, c):
            sel_ref[j] = jnp.where(key_ref[j] >= thr, jnp.where(causal(j), 0.0, NEG_INF), NEG_INF)
            return c

        lax.fori_loop(0, nb, sel_blk, 0)

    def logits(j, p):
        bias = jnp.concatenate([bias_ref[jnp.clip(sub * i + u - j, 0, 2), p] for u in range(sub)], axis=1)
        sel = sel_ref[j]
        s = _dot_nt(k2_ref[j], zq_ref[:, p * LANES:(p + 1) * LANES])
        return s + bias + jnp.concatenate([sel, sel], axis=0)

    def col_max(x):
        return jnp.max(jnp.max(x.reshape(kb // 8, 8, tq), axis=0), axis=0, keepdims=True)

    acc_ref[...] = jnp.zeros(acc_ref.shape, F32)
    quarter = kb // 2

    def att_blk(j, ms):
        out = []
        for p in range(npair):
            s = logits(j, p)
            pes, als, mn = [], [], []
            for u, sl in enumerate((s[:kb], s[kb:])):
                m_old = ms[2 * p + u]
                m_new = jnp.maximum(m_old, col_max(sl))
                m_safe = jnp.where(m_new == NEG_INF, 0.0, m_new)
                pes.append(jnp.exp(sl - m_safe))
                als.append(jnp.broadcast_to(jnp.exp(m_old - m_safe), (quarter, tq)))
                mn.append(m_new)
            pe = jnp.concatenate(pes, axis=0).astype(BF16)
            alpha = jnp.concatenate(als + als, axis=0)
            acc_ref[p] = acc_ref[p] * alpha + _dot(v2t_ref[j], pe)
            out += mn
        return tuple(out)

    lax.fori_loop(0, nb, att_blk, tuple(jnp.full((1, tq), NEG_INF, F32) for _ in range(ATT_HEADS)))
    for p in range(npair):
        a = acc_ref[p]
        o_ref[:, p * LANES:(p + 1) * LANES] = (a[:kb] / a[kb:]).T.astype(o_ref.dtype)


def _t5_bucket(rel):
    max_exact = REL_BUCKETS // 2
    rel_f = jnp.maximum(rel, 1).astype(F32)
    large = max_exact + (jnp.log(rel_f / max_exact) / math.log(REL_MAX_DIST / max_exact)
                         * (REL_BUCKETS - max_exact)).astype(I32)
    large = jnp.minimum(large, REL_BUCKETS - 1)
    return jnp.where(rel < max_exact, rel, large)


def _bias_tiles(rel_bias):
    kb = KEY_BLOCK
    assert kb >= REL_MAX_DIST
    tq = jnp.arange(kb)[:, None]
    sk = jnp.arange(kb)[None, :]
    tiles = []
    for off in range(3):
        rel = jnp.maximum(off * kb + tq - sk, 0)
        bucket = _t5_bucket(rel)[..., None]
        tile = jnp.zeros((kb, kb, rel_bias.shape[1]), F32)
        for b in range(REL_BUCKETS):
            tile = jnp.where(bucket == b, rel_bias[b].astype(F32), tile)
        per_head = jnp.transpose(tile, (2, 1, 0))
        tiles.append(jnp.concatenate([per_head[0::2], per_head[1::2]], axis=-2))
    return jnp.stack(tiles, axis=0).astype(F32)


def _dsa(zq, zs, bias_tiles, bsz, seq_len):
    t = zq.shape[0]
    nq = seq_len // Q_ROWS
    nkb = seq_len // KEY_BLOCK
    k_sel = min(TOPK_MAX, seq_len // 4)
    aw = ATT_HEADS * ATT_DH
    npair = ATT_HEADS // 2
    return pl.pallas_call(
        functools.partial(_dsa_body, k_sel=k_sel),
        grid=(bsz, nq),
        in_specs=[
            pl.BlockSpec((Q_ROWS, zq.shape[1]), lambda b, i: (b * nq + i, 0)),
            pl.BlockSpec((Q_ROWS, zs.shape[1]), lambda b, i: (b * nq + i, 0)),
            pl.BlockSpec((seq_len, zs.shape[1]), lambda b, i: (b, 0)),
            pl.BlockSpec(bias_tiles.shape, lambda b, i: (0, 0, 0, 0)),
        ],
        out_specs=pl.BlockSpec((Q_ROWS, aw), lambda b, i: (b * nq + i, 0)),
        out_shape=jax.ShapeDtypeStruct((t, aw), BF16),
        scratch_shapes=[
            pltpu.VMEM((nkb, KEY_BLOCK, Q_ROWS), I32),
            pltpu.VMEM((nkb, KEY_BLOCK, Q_ROWS), I16),
            pltpu.VMEM((nkb, KEY_BLOCK, Q_ROWS), I16),
            pltpu.VMEM((nkb, KEY_BLOCK, Q_ROWS), F32),
            pltpu.VMEM((nkb, 2 * KEY_BLOCK, LANES), BF16),
            pltpu.VMEM((nkb, 2 * KEY_BLOCK, LANES), BF16),
            pltpu.VMEM((nkb, 2 * LANES, 2 * KEY_BLOCK), BF16),
            pltpu.VMEM((npair, 2 * LANES, Q_ROWS), F32),
        ],
        compiler_params=_params(("arbitrary", "arbitrary")),
        name="dsa",
    )(zq, zs, zs, bias_tiles)


def _merge_body(x_ref, oa_ref, ob_ref, zg_ref, gm_ref, shf_ref, scf_ref, nw_ref,
                wpa, wpb, wout, wrt, br, x1_ref, h2_ref, lgt_ref):
    d = x_ref.shape[1]
    ya = _dot(oa_ref[...], wpa[...])
    yb = _dot(ob_ref[...], wpb[...])
    merged = _sigmoid(zg_ref[:, :d]) * ya + _sigmoid(zg_ref[:, d:]) * yb
    y = _dot(merged.astype(BF16), wout[...])
    x1 = x_ref[...] + gm_ref[0] * y
    x1_ref[...] = x1
    h2 = _rms_mod(x1, nw_ref[...], scf_ref[0], shf_ref[0])
    for j in range(TOK_TILE):
        h2_ref[pl.ds(j, x_ref.shape[0], stride=TOK_TILE), :] = h2[:, j * LANES:(j + 1) * LANES]
    hh, hl = _split2(h2)
    wh, wl = _split2(wrt[...])
    lgt_ref[...] = _dot_nt(wh, hh) + (_dot_nt(wh, hl) + _dot_nt(wl, hh)) + br[...]


def _merge(x2d, oa, ob, zg, gate_m, shift_f, scale_f, nw, wpa, wpb, wout, wrt, br, seq_len):
    t, d = x2d.shape
    assert d == TOK_TILE * LANES
    tm = 256
    per_b = seq_len // tm
    e = wrt.shape[0]
    row = lambda i: (i, 0)
    const = lambda i: (0, 0)
    bmap = lambda i: (i // per_b, 0, 0)
    return pl.pallas_call(
        _merge_body,
        grid=(t // tm,),
        in_specs=[
            pl.BlockSpec((tm, d), row),
            pl.BlockSpec((tm, oa.shape[1]), row),
            pl.BlockSpec((tm, ob.shape[1]), row),
            pl.BlockSpec((tm, zg.shape[1]), row),
            pl.BlockSpec((1, 1, d), bmap),
            pl.BlockSpec((1, 1, d), bmap),
            pl.BlockSpec((1, 1, d), bmap),
            pl.BlockSpec((1, d), const),
            pl.BlockSpec(wpa.shape, const),
            pl.BlockSpec(wpb.shape, const),
            pl.BlockSpec(wout.shape, const),
            pl.BlockSpec(wrt.shape, const),
            pl.BlockSpec(br.shape, const),
        ],
        out_specs=[pl.BlockSpec((tm, d), row), pl.BlockSpec((tm * TOK_TILE, LANES), row),
                   pl.BlockSpec((e, tm), lambda i: (0, i))],
        out_shape=[jax.ShapeDtypeStruct((t, d), F32), jax.ShapeDtypeStruct((t * TOK_TILE, LANES), F32),
                   jax.ShapeDtypeStruct((e, t), F32)],
        compiler_params=_params(("parallel",)),
        name="merge",
    )(x2d, oa, ob, zg, gate_m, shift_f, scale_f, nw, wpa, wpb, wout, wrt, br)


def _route1_body(lg_ref, er_ref, gt_ref, cnt_ref, base_ref):
    i = pl.program_id(0)
    e, tr = lg_ref.shape

    @pl.when(i == 0)
    def _():
        base_ref[...] = jnp.zeros_like(base_ref)

    l = lg_ref[...]
    eidx = lax.broadcasted_iota(I32, (e, tr), 0).astype(F32)
    vals, es, ohs = [], [], []
    for _ in range(TOP_K):
        m = jnp.max(l, axis=0, keepdims=True)
        ek = jnp.min(jnp.where(l == m, eidx, float(e)), axis=0, keepdims=True)
        oh = eidx == ek
        l = jnp.where(oh, NEG_INF, l)
        vals.append(m)
        es.append(ek)
        ohs.append(oh)
    ps = [jnp.exp(v - vals[0]) for v in vals]
    den = ps[0] + ps[1] + ps[2] + ps[3]
    gs = [p / den for p in ps]

    member = jnp.zeros((e, tr), F32)
    for oh in ohs:
        member = member + jnp.where(oh, 1.0, 0.0)
    mb = member.astype(BF16)
    r2 = lax.broadcasted_iota(I32, (tr, tr), 0)
    c2 = lax.broadcasted_iota(I32, (tr, tr), 1)
    sut = jnp.where(r2 < c2, 1.0, 0.0).astype(BF16)
    base = base_ref[...]
    pref = _dot(mb, sut) + jnp.concatenate([base] * (tr // LANES), axis=1)
    ranks = [jnp.sum(jnp.where(oh, pref, 0.0), axis=0, keepdims=True) for oh in ohs]
    er_ref[...] = jnp.concatenate(es + ranks, axis=0).astype(I32)

    row8 = lax.broadcasted_iota(I32, (8, tr), 0)
    g8 = jnp.zeros((8, tr), F32)
    for kk in range(TOP_K):
        g8 = jnp.where(row8 == kk, gs[kk], g8)
    gfull = jnp.concatenate([g8, jnp.zeros((LANES - 8, tr), F32)], axis=0)
    gt_ref[...] = gfull.T

    new_base = base + _dot(mb, jnp.ones((tr, LANES), BF16))
    base_ref[...] = new_base
    cnt_ref[...] = new_base


def _route1(lgt):
    e, t = lgt.shape
    tr = 512
    return pl.pallas_call(
        _route1_body,
        grid=(t // tr,),
        in_specs=[pl.BlockSpec((e, tr), lambda i: (0, i))],
        out_specs=[pl.BlockSpec((2 * TOP_K, tr), lambda i: (0, i)),
                   pl.BlockSpec((tr, LANES), lambda i: (i, 0)),
                   pl.BlockSpec((e, LANES), lambda i: (0, 0))],
        out_shape=[jax.ShapeDtypeStruct((2 * TOP_K, t), I32),
                   jax.ShapeDtypeStruct((t, LANES), F32),
                   jax.ShapeDtypeStruct((e, LANES), F32)],
        scratch_shapes=[pltpu.VMEM((e, LANES), F32)],
        compiler_params=_params(("arbitrary",)),
        name="route_topk",
    )(lgt)


def _route2_body(er_ref, cnt_ref, slot_ref, be_ref, tot_ref, fs_ref):
    e = cnt_ref.shape[0]
    tt = er_ref.shape[1]
    nbp = be_ref.shape[1]
    sh = int(math.log2(MOE_BLOCK))
    cnt = cnt_ref[...].astype(I32)
    padded = ((cnt + (MOE_BLOCK - 1)) >> sh) << sh
    rows = []
    acc = jnp.zeros((1, LANES), I32)
    for ei in range(e):
        rows.append(acc)
        acc = acc + padded[ei:ei + 1, :]
    pstart = jnp.concatenate(rows, axis=0)
    pend = pstart + padded
    tot_ref[...] = jnp.broadcast_to(acc, tot_ref.shape)
    fs_ref[...] = jnp.where(padded > 0, pend - MOE_BLOCK, -1)

    er = er_ref[...]
    eidx = lax.broadcasted_iota(I32, (e, tt), 0)
    pst = jnp.concatenate([pstart] * (tt // LANES), axis=1)
    outs = []
    for kk in range(TOP_K):
        ps = jnp.sum(jnp.where(eidx == er[kk:kk + 1, :], pst, 0), axis=0, keepdims=True)
        outs.append(ps + er[TOP_K + kk:TOP_K + kk + 1, :])
    slot_ref[...] = jnp.concatenate(outs + [jnp.zeros((TOP_K, tt), I32)], axis=0)

    blk0 = lax.broadcasted_iota(I32, (e, nbp), 1) * MOE_BLOCK
    pe = jnp.concatenate([pend] * (nbp // LANES), axis=1)
    nle = jnp.sum(jnp.where(pe <= blk0, 1, 0), axis=0, keepdims=True)
    be_ref[...] = jnp.broadcast_to(jnp.minimum(nle, e - 1), be_ref.shape)


def _route2(er, cnt, nbp):
    t = er.shape[1]
    tt = min(2048, t)
    e = cnt.shape[0]
    return pl.pallas_call(
        _route2_body,
        grid=(t // tt,),
        in_specs=[pl.BlockSpec((2 * TOP_K, tt), lambda i: (0, i)),
                  pl.BlockSpec((e, LANES), lambda i: (0, 0))],
        out_specs=[pl.BlockSpec((2 * TOP_K, tt), lambda i: (0, i)),
                   pl.BlockSpec((8, nbp), lambda i: (0, 0)),
                   pl.BlockSpec((8, LANES), lambda i: (0, 0)),
                   pl.BlockSpec((e, LANES), lambda i: (0, 0))],
        out_shape=[jax.ShapeDtypeStruct((2 * TOP_K, t), I32),
                   jax.ShapeDtypeStruct((8, nbp), I32),
                   jax.ShapeDtypeStruct((8, LANES), I32),
                   jax.ShapeDtypeStruct((e, LANES), I32)],
        compiler_params=_params(("arbitrary",)),
        name="route_slots",
    )(er, cnt)


def _tile(row):
    return pl.ds(pl.multiple_of(row * TOK_TILE, TOK_TILE), TOK_TILE)


def _dispatch_body(fs_ref, nu_ref, slot_ref, h_ref, xs_ref, zbuf, sem, zsem):
    i = pl.program_id(0)
    td = slot_ref.shape[1]

    @pl.when(i == 0)
    def _():
        zbuf[...] = jnp.zeros_like(zbuf)
        zrows = zbuf.shape[0]
        nblk = xs_ref.shape[0] // zrows
        fills = []
        for e in range(fs_ref.shape[0]):
            fills.append((fs_ref[e] >= 0, jnp.maximum(fs_ref[e], 0) * TOK_TILE))
        for b in range(fs_ref.shape[0]):
            blk = nu_ref[0] + b
            fills.append((blk < nblk, jnp.minimum(blk, nblk - 1) * zrows))
        for pred, start in fills:
            @pl.when(pred)
            def _(start=start):
                pltpu.make_async_copy(zbuf, xs_ref.at[pl.ds(pl.multiple_of(start, zrows), zrows)], zsem).start()
        for pred, _ in fills:
            @pl.when(pred)
            def _():
                pltpu.make_async_copy(zbuf, xs_ref.at[pl.ds(0, zrows)], zsem).wait()

    def issue(g, c):
        for u in range(DMA_UNROLL):
            tok = g * DMA_UNROLL + u
            for kk in range(TOP_K):
                pltpu.make_async_copy(h_ref.at[_tile(tok)], xs_ref.at[_tile(slot_ref[kk, tok])], sem).start(
                    priority=(u * TOP_K + kk) % 2)
        return c

    lax.fori_loop(0, td // DMA_UNROLL, issue, 0)
    for kk in range(TOP_K):
        pltpu.make_async_copy(h_ref, xs_ref.at[pl.ds(0, td * TOK_TILE)], sem).wait()


def _dispatch(fill_start, nused, slot, h2, n_slots):
    t = h2.shape[0] // TOK_TILE
    td = 512
    grid_spec = pltpu.PrefetchScalarGridSpec(
        num_scalar_prefetch=2,
        grid=(t // td,),
        in_specs=[pl.BlockSpec((2 * TOP_K, td), lambda i, fs, nu: (0, i), memory_space=pltpu.SMEM),
                  pl.BlockSpec((td * TOK_TILE, LANES), lambda i, fs, nu: (i, 0))],
        out_specs=pl.BlockSpec(memory_space=pl.ANY),
        scratch_shapes=[pltpu.VMEM((MOE_BLOCK * TOK_TILE, LANES), h2.dtype), pltpu.SemaphoreType.DMA, pltpu.SemaphoreType.DMA],
    )
    return pl.pallas_call(
        _dispatch_body,
        grid_spec=grid_spec,
        out_shape=jax.ShapeDtypeStruct((n_slots * TOK_TILE, LANES), h2.dtype),
        compiler_params=_params(("arbitrary",)),
        name="moe_dispatch",
    )(fill_start, nused, slot, h2)


def _expert_body(be_ref, nu_ref, xs_ref, w1_ref, b1_ref, w2_ref, b2_ref, y_ref, w1b, w2b):
    i = pl.program_id(0)
    f = w2_ref.shape[2]

    @pl.when(jnp.logical_or(i == 0, be_ref[i] != be_ref[jnp.maximum(i - 1, 0)]))
    def _():
        w1b[...] = w1_ref[0, 0].astype(BF16)
        w2b[...] = w2_ref[0, 0].astype(BF16)

    @pl.when(i < nu_ref[0])
    def _():
        x = jnp.concatenate([xs_ref[pl.ds(j, MOE_BLOCK, stride=TOK_TILE), :] for j in range(TOK_TILE)], axis=1)
        hm = _dot(x.astype(BF16), w1b[...]) + b1_ref[0, 0]
        glu = jnp.minimum(hm[:, :f], SWIGLU_LIMIT)
        lin = jnp.clip(hm[:, f:], -SWIGLU_LIMIT, SWIGLU_LIMIT)
        act = glu * _sigmoid(SWIGLU_ALPHA * glu) * (lin + 1.0)
        y = _dot(act.astype(BF16), w2b[...]) + b2_ref[0, 0]
        for j in range(TOK_TILE):
            y_ref[pl.ds(j, MOE_BLOCK, stride=TOK_TILE), :] = y[:, j * LANES:(j + 1) * LANES]

    @pl.when(i >= nu_ref[0])
    def _():
        y_ref[...] = jnp.zeros_like(y_ref)


def _experts(be, nused, xs, w1, b1, w2, b2, layer):
    ns = xs.shape[0] // TOK_TILE
    depth, e, d, f2 = w1.shape
    f = w2.shape[2]
    nblk = ns // MOE_BLOCK
    used = lambda i, nu: jnp.minimum(i, nu[0] - 1)
    grid_spec = pltpu.PrefetchScalarGridSpec(
        num_scalar_prefetch=2,
        grid=(nblk,),
        in_specs=[
            pl.BlockSpec((MOE_BLOCK * TOK_TILE, LANES), lambda i, be, nu: (used(i, nu), 0)),
            pl.BlockSpec((1, 1, d, f2), lambda i, be, nu: (layer, be[i], 0, 0)),
            pl.BlockSpec((1, 1, 1, f2), lambda i, be, nu: (layer, be[i], 0, 0)),
            pl.BlockSpec((1, 1, f, d), lambda i, be, nu: (layer, be[i], 0, 0)),
            pl.BlockSpec((1, 1, 1, d), lambda i, be, nu: (layer, be[i], 0, 0)),
        ],
        out_specs=pl.BlockSpec((MOE_BLOCK * TOK_TILE, LANES), lambda i, be, nu: (i, 0)),
        scratch_shapes=[pltpu.VMEM((d, f2), BF16), pltpu.VMEM((f, d), BF16)],
    )
    return pl.pallas_call(
        _expert_body,
        grid_spec=grid_spec,
        out_shape=jax.ShapeDtypeStruct((ns * TOK_TILE, LANES), F32),
        compiler_params=_params(("arbitrary",)),
        name="moe_experts",
    )(be, nused, xs, w1, b1.reshape(depth, e, 1, f2), w2, b2.reshape(depth, e, 1, d))


def _combine_body(slot_ref, y_ref, x1_ref, gt_ref, gf_ref, fw_ref, o_ref, buf, sem, *, final):
    td = x1_ref.shape[0]

    def issue(g, c):
        for u in range(DMA_UNROLL):
            tok = g * DMA_UNROLL + u
            for kk in range(TOP_K):
                pltpu.make_async_copy(y_ref.at[_tile(slot_ref[kk, tok])], buf.at[_tile(kk * td + tok)], sem).start(
                    priority=(u * TOP_K + kk) % 2)
        return c

    lax.fori_loop(0, td // DMA_UNROLL, issue, 0)
    for kk in range(TOP_K):
        pltpu.make_async_copy(y_ref.at[pl.ds(0, td * TOK_TILE)], buf.at[pl.ds(kk * td * TOK_TILE, td * TOK_TILE)], sem).wait()
    g = gt_ref[...]
    gk = [g[:, kk:kk + 1] for kk in range(TOP_K)]
    parts = []
    for j in range(TOK_TILE):
        y = gk[0] * buf[pl.ds(j, td, stride=TOK_TILE), :]
        for kk in range(1, TOP_K):
            y = y + gk[kk] * buf[pl.ds(kk * td * TOK_TILE + j, td, stride=TOK_TILE), :]
        parts.append(x1_ref[:, j * LANES:(j + 1) * LANES] + gf_ref[0][:, j * LANES:(j + 1) * LANES] * y)
    x2 = jnp.concatenate(parts, axis=-1)
    if final:
        ms = jnp.mean(x2 * x2, axis=-1, keepdims=True)
        x2 = x2 * lax.rsqrt(ms + NORM_EPS) * fw_ref[...]
    o_ref[...] = x2


def _combine(slot, ys, x1, gt, gate_f, fw, seq_len, final):
    t, d = x1.shape
    td = 256
    per_b = seq_len // td
    return pl.pallas_call(
        functools.partial(_combine_body, final=final),
        grid=(t // td,),
        in_specs=[pl.BlockSpec((2 * TOP_K, td), lambda i: (0, i), memory_space=pltpu.SMEM),
                  pl.BlockSpec(memory_space=pl.ANY),
                  pl.BlockSpec((td, d), lambda i: (i, 0)),
                  pl.BlockSpec((td, LANES), lambda i: (i, 0)),
                  pl.BlockSpec((1, 1, d), lambda i: (i // per_b, 0, 0)),
                  pl.BlockSpec((1, d), lambda i: (0, 0))],
        out_specs=pl.BlockSpec((td, d), lambda i: (i, 0)),
        out_shape=jax.ShapeDtypeStruct((t, d), F32),
        scratch_shapes=[pltpu.VMEM((TOP_K * td * TOK_TILE, LANES), F32), pltpu.SemaphoreType.DMA],
        compiler_params=_params(("arbitrary",)),
        name="moe_combine",
    )(slot, ys, x1, gt, gate_f, fw)


def _split_w_in(w):
    hw = HGRN_HEADS * HGRN_DK
    aw = ATT_HEADS * ATT_DH
    iw = IDX_HEADS * IDX_DH
    d = w.shape[0]
    o = 4 * hw
    qb = w[:, o:o + aw]
    kb = w[:, o + aw:o + aw + ATT_DH]
    vb = w[:, o + aw + ATT_DH:o + aw + 2 * ATT_DH]
    o2 = o + aw + 2 * ATT_DH
    iq = w[:, o2:o2 + iw]
    ik = w[:, o2 + iw:o2 + iw + IDX_DH]
    ih = w[:, o2 + iw + IDX_DH:o2 + iw + IDX_DH + IDX_HEADS]
    o3 = o2 + iw + IDX_DH + IDX_HEADS
    small = jnp.concatenate([kb, vb, ik, ih, jnp.zeros((d, 2 * LANES - 3 * ATT_DH - IDX_HEADS), w.dtype)], axis=1)
    groups = (w[:, :o], jnp.concatenate([qb, iq], axis=1), small, w[:, o3:])
    return tuple(g.astype(BF16) for g in groups)


def kernel(x, c, w_ada, b_ada, norm_mix_w, w_in, hgrn_lb_logits, hgrn_norm_w, rel_bias, w_proj_a, w_proj_b,
           w_out, norm_ffn_w, w_router, b_router, w_mlp1, b_mlp1, w_mlp2, b_mlp2, final_norm_w):
    bsz, seq_len, d = x.shape
    depth = w_in.shape[0]
    n_exp = w_router.shape[2]
    t = bsz * seq_len
    n_slots = t * TOP_K + n_exp * MOE_BLOCK
    nblk = n_slots // MOE_BLOCK
    nbp = -(-nblk // LANES) * LANES

    mod = _ada(c, w_ada, b_ada)
    bias_tiles = _bias_tiles(rel_bias)
    x2d = x.reshape(t, d)
    for l in range(depth):
        m6 = mod[l].reshape(bsz, N_MOD, 1, d)
        shift_m, scale_m, gate_m, shift_f, scale_f, gate_f = (m6[:, n] for n in range(N_MOD))
        zh, zq, zs, zg = _inproj(x2d, shift_m, scale_m, norm_mix_w[l].reshape(1, d), _split_w_in(w_in[l]), seq_len)
        oa = _hgrn(zh, hgrn_lb_logits, hgrn_norm_w[l], l, bsz, seq_len).reshape(t, -1)
        ob = _dsa(zq, zs, bias_tiles, bsz, seq_len)
        x1, h2, lgt = _merge(x2d, oa, ob, zg, gate_m, shift_f, scale_f, norm_ffn_w[l].reshape(1, d),
                             w_proj_a[l].astype(BF16), w_proj_b[l].astype(BF16), w_out[l].astype(BF16),
                             w_router[l].T, b_router[l].reshape(n_exp, 1), seq_len)
        er, gt, cnt = _route1(lgt)
        slot, be, tot, fill_start = _route2(er, cnt, nbp)
        nused = (tot[0, :1] >> int(math.log2(MOE_BLOCK))).astype(I32)
        xs = _dispatch(fill_start[:, 0], nused, slot, h2, n_slots)
        ys = _experts(be[0, :nblk], nused, xs, w_mlp1, b_mlp1, w_mlp2, b_mlp2, l)
        x2d = _combine(slot, ys, x1, gt, gate_f, final_norm_w.reshape(1, d), seq_len, final=(l == depth - 1))
    return x2d.reshape(bsz, seq_len, d)
```

```python
import functools
import math

import jax
import jax.numpy as jnp
from jax import lax
from jax.experimental import pallas as pl
from jax.experimental.pallas import tpu as pltpu

F32 = jnp.float32
BF16 = jnp.bfloat16
I32 = jnp.int32
I16 = jnp.int16

HGRN_HEADS = 4
HGRN_DK = 128
ATT_HEADS = 8
ATT_DH = 64
IDX_HEADS = 8
IDX_DH = 64
TOPK_MAX = 256
REL_BUCKETS = 32
REL_MAX_DIST = 128
TOP_K = 4
SWIGLU_ALPHA = 1.702
SWIGLU_LIMIT = 7.0
NORM_EPS = 1e-6
N_MOD = 6

LANES = 128
HGRN_CHUNK = 128
HGRN_SUB = 4
HGRN_PAIR = 4
KEY_BLOCK = 128
Q_ROWS = 256
MOE_BLOCK = 256
DMA_UNROLL = 8
TOK_TILE = 8
VMEM_LIMIT = 56 * 1024 * 1024
NEG_INF = float("-inf")
INT_MIN = -(2 ** 31)


def _dot(a, b):
    return jnp.dot(a, b, preferred_element_type=F32)


def _dot_nt(a, b):
    return lax.dot_general(a, b, (((1,), (1,)), ((), ())), preferred_element_type=F32)


def _dot_tn(a, b):
    return lax.dot_general(a, b, (((0,), (0,)), ((), ())), preferred_element_type=F32)


def _split2(a):
    hi = a.astype(BF16)
    lo = (a - hi.astype(F32)).astype(BF16)
    return hi, lo


def _split3(a):
    hi = a.astype(BF16)
    r = a - hi.astype(F32)
    mid = r.astype(BF16)
    lo = (r - mid.astype(F32)).astype(BF16)
    return hi, mid, lo


def _sigmoid(x):
    return 0.5 * jnp.tanh(0.5 * x) + 0.5


def _params(sem):
    return pltpu.CompilerParams(dimension_semantics=sem, vmem_limit_bytes=VMEM_LIMIT)


def _ada_body(c_ref, w_ref, b_ref, o_ref):
    c = c_ref[...]
    ca = c * _sigmoid(c)
    o_ref[0] = _dot(ca, w_ref[0]) + b_ref[0]


def _ada(c, w_ada, b_ada):
    depth, d, nd = w_ada.shape
    bsz = c.shape[0]
    tn = 1024
    return pl.pallas_call(
        _ada_body,
        grid=(depth, nd // tn),
        in_specs=[
            pl.BlockSpec((bsz, d), lambda l, j: (0, 0)),
            pl.BlockSpec((1, d, tn), lambda l, j: (l, 0, j)),
            pl.BlockSpec((1, 1, tn), lambda l, j: (l, 0, j)),
        ],
        out_specs=pl.BlockSpec((1, bsz, tn), lambda l, j: (l, 0, j)),
        out_shape=jax.ShapeDtypeStruct((depth, bsz, nd), F32),
        compiler_params=_params(("parallel", "parallel")),
        name="ada_mod",
    )(c, w_ada, b_ada.reshape(depth, 1, nd))


def _rms_mod(x, nw, scale, shift):
    ms = jnp.mean(x * x, axis=-1, keepdims=True)
    y = x * lax.rsqrt(ms + NORM_EPS) * nw
    return y * (1.0 + scale) + shift


def _inproj_body(x_ref, sh_ref, sc_ref, nw_ref, w1, w2, w3, w4, zh, zq, zs, zg):
    h = _rms_mod(x_ref[...], nw_ref[...], sc_ref[0], sh_ref[0]).astype(BF16)
    zh[...] = _dot(h, w1[...])
    zq[...] = _dot(h, w2[...]).astype(BF16)
    zs[...] = _dot(h, w3[...])
    zg[...] = _dot(h, w4[...])


def _inproj(x2d, shift, scale, nw, ws, seq_len):
    t, d = x2d.shape
    tm = 256
    per_b = seq_len // tm
    w1, w2, w3, w4 = ws
    row = lambda i: (i, 0)
    const = lambda i: (0, 0)
    bmap = lambda i: (i // per_b, 0, 0)
    outs = [
        jax.ShapeDtypeStruct((t, w1.shape[1]), F32),
        jax.ShapeDtypeStruct((t, w2.shape[1]), BF16),
        jax.ShapeDtypeStruct((t, w3.shape[1]), F32),
        jax.ShapeDtypeStruct((t, w4.shape[1]), F32),
    ]
    return pl.pallas_call(
        _inproj_body,
        grid=(t // tm,),
        in_specs=[
            pl.BlockSpec((tm, d), row),
            pl.BlockSpec((1, 1, d), bmap),
            pl.BlockSpec((1, 1, d), bmap),
            pl.BlockSpec((1, d), const),
            pl.BlockSpec(w1.shape, const),
            pl.BlockSpec(w2.shape, const),
            pl.BlockSpec(w3.shape, const),
            pl.BlockSpec(w4.shape, const),
        ],
        out_specs=[pl.BlockSpec((tm, o.shape[1]), row) for o in outs],
        out_shape=outs,
        compiler_params=_params(("parallel",)),
        name="in_proj",
    )(x2d, shift, scale, nw, w1, w2, w3, w4)


def _hgrn_body(lbl_ref, qa_ref, fa_ref, ia_ref, ga_ref, nw_ref, o_ref, *, layer, seq_len):
    C, R = HGRN_CHUNK, HGRN_SUB
    nh = qa_ref.shape[2] // LANES
    lg = lbl_ref[...]
    ex = jnp.exp(lg - jnp.max(lg, axis=0, keepdims=True))
    sm = ex / jnp.sum(ex, axis=0, keepdims=True)
    lb_all = jnp.zeros((1, nh * LANES), F32)
    for li in range(1, layer + 1):
        lb_all = lb_all + sm[li]
    nw_all = nw_ref[...]

    rowi = lax.broadcasted_iota(I32, (C, LANES), 0)
    ri2 = lax.broadcasted_iota(I32, (C, C), 0)
    ci2 = lax.broadcasted_iota(I32, (C, C), 1)
    ltri = jnp.where(ci2 <= ri2, 1.0, 0.0).astype(BF16)
    ones_bf = jnp.ones((LANES, LANES), BF16)
    dlt = ri2 - ci2

    def one_head(qa, z, v, ga, st, lb, nw):
        log_lb = jnp.log(lb)
        log_1m = jnp.log1p(-lb)
        q = qa * _sigmoid(qa)
        ls = jnp.minimum(z, 0.0) - jnp.log1p(jnp.exp(-jnp.abs(z)))
        t2 = log_1m + ls
        mx = jnp.maximum(log_lb, t2)
        g = mx + jnp.log(jnp.exp(log_lb - mx) + jnp.exp(t2 - mx))
        k = (1.0 - lb) * _sigmoid(-z)
        g1, g2, g3 = _split3(g)
        b = _dot(ltri, g1) + (_dot(ltri, g2) + _dot(ltri, g3))

        a = jnp.zeros((C, C), F32)
        half = C // 2
        while half >= R:
            span = 2 * half
            nb = C // span
            refb = jnp.concatenate(
                [jnp.broadcast_to(b[p * span + half - 1:p * span + half, :], (span, LANES)) for p in range(nb)],
                axis=0)
            second = (rowi & (span - 1)) >= half
            qt = q * jnp.exp(jnp.where(second, b - refb, NEG_INF))
            kt = k * jnp.exp(jnp.where(second, NEG_INF, refb - b))
            al = _dot_nt(qt.astype(BF16), kt.astype(BF16))
            sh = int(math.log2(span))
            a = a + jnp.where((ri2 >> sh) == (ci2 >> sh), al, 0.0)
            half //= 2

        ps = []
        for dl in range(R):
            if dl == 0:
                kd, bd = k, b
            else:
                kd = pltpu.roll(k, dl, 0)
                bd = pltpu.roll(b, dl, 0)
            arg = jnp.where((rowi & (R - 1)) >= dl, b - bd, NEG_INF)
            ps.append((q * kd * jnp.exp(arg)).astype(BF16))
        rs = _dot(jnp.concatenate(ps, axis=0), ones_bf)
        ad = jnp.zeros((C, C), F32)
        for dl in range(R):
            ad = jnp.where(dlt == dl, rs[dl * C:(dl + 1) * C, :], ad)
        shr = int(math.log2(R))
        a = a + jnp.where((ri2 >> shr) == (ci2 >> shr), ad, 0.0)

        vb = v.astype(BF16)
        o = _dot(a.astype(BF16), vb) + _dot_nt((q * jnp.exp(b)).astype(BF16), st.astype(BF16))
        bl = b[C - 1:C, :]
        kdec = (k * jnp.exp(bl - b)).astype(BF16)
        st_new = st * jnp.exp(bl) + _dot_tn(vb, kdec)
        on = o * lax.rsqrt(jnp.mean(o * o, axis=-1, keepdims=True) + NORM_EPS) * nw
        return (on * (ga * _sigmoid(ga))).astype(o_ref.dtype), st_new

    def chunk(ci, sts):
        rows = pl.ds(pl.multiple_of(ci * C, C), C)
        new = []
        for hd in range(nh):
            ln = slice(hd * LANES, (hd + 1) * LANES)
            out, st_new = one_head(qa_ref[0, rows, ln], fa_ref[0, rows, ln], ia_ref[0, rows, ln], ga_ref[0, rows, ln],
                                   sts[hd], lb_all[:, ln], nw_all[:, ln])
            o_ref[0, rows, ln] = out
            new.append(st_new)
        return tuple(new)

    lax.fori_loop(0, seq_len // C, chunk, tuple(jnp.zeros((LANES, LANES), F32) for _ in range(nh)))


def _hgrn(zh, lb_logits, norm_w, layer, bsz, seq_len):
    depth = lb_logits.shape[0]
    hw = HGRN_HEADS * HGRN_DK
    wb = HGRN_PAIR * LANES
    ng = HGRN_HEADS // HGRN_PAIR
    z3 = zh.reshape(bsz, seq_len, zh.shape[1])
    blk = lambda off: pl.BlockSpec((1, seq_len, wb), lambda b, h: (b, 0, off + h))
    return pl.pallas_call(
        functools.partial(_hgrn_body, layer=layer, seq_len=seq_len),
        grid=(bsz, ng),
        in_specs=[
            pl.BlockSpec((depth, 1, wb), lambda b, h: (0, 0, h)),
            blk(0), blk(ng), blk(2 * ng), blk(3 * ng),
            pl.BlockSpec((1, wb), lambda b, h: (0, h)),
        ],
        out_specs=pl.BlockSpec((1, seq_len, wb), lambda b, h: (b, 0, h)),
        out_shape=jax.ShapeDtypeStruct((bsz, seq_len, hw), BF16),
        compiler_params=_params(("parallel", "parallel")),
        name="hgrn2",
    )(lb_logits.reshape(depth, 1, hw), z3, z3, z3, z3, norm_w.reshape(1, hw))


def _float_key(x):
    x = jnp.where(x == 0.0, 0.0, x)
    bits = lax.bitcast_convert_type(x, I32)
    return bits ^ ((bits >> 31) & 0x7FFFFFFF)


def _dsa_body(zq_ref, zsq_ref, zs_ref, bias_ref, o_ref,
              key_ref, hi_ref, lo_ref, sel_ref, k2_ref, ik2_ref, v2t_ref, acc_ref, *, k_sel):
    i = pl.program_id(1)
    tq, kb = Q_ROWS, KEY_BLOCK
    sub = tq // kb
    nkb = key_ref.shape[0]
    aw = ATT_HEADS * ATT_DH
    npair = ATT_HEADS // 2
    half = LANES // 2
    nb = sub * (i + 1)

    @pl.when(i == 0)
    def _():
        lo = lax.broadcasted_iota(I32, (kb, LANES), 1) < half
        one = jnp.where(lo, 1.0, 0.0)

        def build(j, c):
            r = pl.multiple_of(j * kb, kb)
            kv = zs_ref[pl.ds(r, kb), 0:LANES]
            ix = zs_ref[pl.ds(r, kb), LANES:2 * LANES]
            vk = pltpu.roll(kv, half, 1)
            xi = pltpu.roll(ix, half, 1)
            sc = ATT_DH ** -0.5
            k2_ref[j] = jnp.concatenate([jnp.where(lo, kv * sc, 0.0), jnp.where(lo, 0.0, vk * sc)], axis=0).astype(BF16)
            ik2_ref[j] = jnp.concatenate([jnp.where(lo, ix, 0.0), jnp.where(lo, 0.0, xi)], axis=0).astype(BF16)
            top = jnp.concatenate([jnp.where(lo, vk, 0.0), one], axis=1)
            bot = jnp.concatenate([jnp.where(lo, 0.0, kv), 1.0 - one], axis=1)
            v2t_ref[j] = jnp.concatenate([top, bot], axis=0).T.astype(BF16)
            return c

        lax.fori_loop(0, nkb, build, 0)

    krow = lax.broadcasted_iota(I32, (kb, tq), 0)
    qcol = lax.broadcasted_iota(I32, (kb, tq), 1)
    r1 = lax.broadcasted_iota(I32, (LANES, LANES), 0)
    c1 = lax.broadcasted_iota(I32, (LANES, LANES), 1)
    slt = jnp.where(c1 < r1, 1.0, 0.0).astype(BF16)
    iwt = (zsq_ref[:, LANES:2 * LANES] * (IDX_HEADS ** -0.5 * IDX_DH ** -0.5)).T

    def causal(j):
        return (kb * j + krow) <= (tq * i + qcol)

    def score_one(j):
        acc = jnp.zeros((kb, tq), F32)
        for p in range(IDX_HEADS // 2):
            s = _dot_nt(ik2_ref[j], zq_ref[:, aw + p * LANES:aw + (p + 1) * LANES])
            w0 = iwt[half + 2 * p:half + 2 * p + 1, :]
            w1 = iwt[half + 2 * p + 1:half + 2 * p + 2, :]
            acc = acc + (w0 * jnp.maximum(s[:kb], 0.0) + w1 * jnp.maximum(s[kb:], 0.0))
        key = _float_key(jnp.where(causal(j), acc, NEG_INF))
        key_ref[j] = key
        hi_ref[j] = (key >> 16).astype(I16)

    def score_grp(g, c):
        for u in range(sub):
            score_one(sub * g + u)
        return c

    lax.fori_loop(0, i + 1, score_grp, 0)

    kf = float(k_sel)

    def count(pred):
        def grp(g, acc):
            for u in range(sub):
                hit = jnp.where(pred(key_ref[sub * g + u]), 1.0, 0.0)
                acc = acc + jnp.sum(hit.reshape(kb // 8, 8, tq), axis=0)
            return acc
        acc = lax.fori_loop(0, i + 1, grp, jnp.zeros((8, tq), F32))
        return jnp.sum(acc, axis=0, keepdims=True)

    def count16(ref, pred):
        def grp(g, acc):
            for u in range(sub):
                hit = jnp.where(pred(ref[sub * g + u]), jnp.int16(1), jnp.int16(0))
                for r in range(kb // 16):
                    acc = acc + hit[16 * r:16 * (r + 1), :]
            return acc
        acc = lax.fori_loop(0, i + 1, grp, jnp.zeros((16, tq), I16))
        return jnp.sum(acc.astype(F32), axis=0, keepdims=True)

    def kth16(ref, base):
        zero16 = jnp.zeros((1, tq), I16)
        t0 = jnp.where(base + count16(ref, lambda v: v >= zero16) >= kf, 0, -32768).astype(I32)

        def step(it, t):
            cand = t | jnp.left_shift(1, 14 - it)
            c16 = cand.astype(I16)
            return jnp.where(base + count16(ref, lambda v: v >= c16) >= kf, cand, t)

        return lax.fori_loop(0, 15, step, t0)

    t_hi = kth16(hi_ref, 0.0)
    h16 = t_hi.astype(I16)
    above = count16(hi_ref, lambda v: v > h16)

    def low_blk(j, c):
        lo = ((key_ref[j] & 0xFFFF) - 32768).astype(I16)
        lo_ref[j] = jnp.where(hi_ref[j] == h16, lo, jnp.int16(-32768))
        return c

    lax.fori_loop(0, nb, low_blk, 0)
    t_lo = kth16(lo_ref, above)
    thr = (t_hi << 16) | ((t_lo + 32768) & 0xFFFF)
    need = kf - count(lambda kj: kj > thr)

    excess = jnp.max(count(lambda kj: kj == thr) - need) > 0.5

    @pl.when(excess)
    def _():
        def sel_blk(j, carry):
            kj = key_ref[j]
            eq = jnp.where(kj == thr, 1.0, 0.0)
            pref = _dot(slt, eq.astype(BF16)) + carry
            tie_ok = jnp.where(kj == thr, jnp.where(pref < need, 1.0, 0.0), 0.0)
            chosen = jnp.where(kj > thr, 1.0, tie_ok)
            chosen = jnp.where(causal(j), chosen, 0.0)
            sel_ref[j] = jnp.where(chosen > 0.0, 0.0, NEG_INF)
            return carry + jnp.sum(eq, axis=0, keepdims=True)

        lax.fori_loop(0, nb, sel_blk, jnp.zeros((1, tq), F32))

    @pl.when(jnp.logical_not(excess))
    def _():
        def sel_blk(j, c):
            sel_ref[j] = jnp.where(key_ref[j] >= thr, jnp.where(causal(j), 0.0, NEG_INF), NEG_INF)
            return c

        lax.fori_loop(0, nb, sel_blk, 0)

    def logits(j, p):
        bias = jnp.concatenate([bias_ref[jnp.clip(sub * i + u - j, 0, 2), p] for u in range(sub)], axis=1)
        sel = sel_ref[j]
        s = _dot_nt(k2_ref[j], zq_ref[:, p * LANES:(p + 1) * LANES])
        return s + bias + jnp.concatenate([sel, sel], axis=0)

    def col_max(x):
        return jnp.max(jnp.max(x.reshape(kb // 8, 8, tq), axis=0), axis=0, keepdims=True)

    acc_ref[...] = jnp.zeros(acc_ref.shape, F32)
    quarter = kb // 2

    def att_grp(g, ms):
        out = []
        for p in range(npair):
            ss = [logits(sub * g + u, p) for u in range(sub)]
            pes = [[None, None] for _ in range(sub)]
            als, mn = [], []
            for hd in range(2):
                sl = [x[hd * kb:(hd + 1) * kb] for x in ss]
                m_old = ms[2 * p + hd]
                m_new = m_old
                for x in sl:
                    m_new = jnp.maximum(m_new, col_max(x))
                m_safe = jnp.where(m_new == NEG_INF, 0.0, m_new)
                for u in range(sub):
                    pes[u][hd] = jnp.exp(sl[u] - m_safe)
                als.append(jnp.broadcast_to(jnp.exp(m_old - m_safe), (quarter, tq)))
                mn.append(m_new)
            upd = acc_ref[p] * jnp.concatenate(als + als, axis=0)
            for u in range(sub):
                upd = upd + _dot(v2t_ref[sub * g + u], jnp.concatenate(pes[u], axis=0).astype(BF16))
            acc_ref[p] = upd
            out += mn
        return tuple(out)

    lax.fori_loop(0, i + 1, att_grp, tuple(jnp.full((1, tq), NEG_INF, F32) for _ in range(ATT_HEADS)))
    for p in range(npair):
        a = acc_ref[p]
        o_ref[:, p * LANES:(p + 1) * LANES] = (a[:kb] / a[kb:]).T.astype(o_ref.dtype)


def _t5_bucket(rel):
    max_exact = REL_BUCKETS // 2
    rel_f = jnp.maximum(rel, 1).astype(F32)
    large = max_exact + (jnp.log(rel_f / max_exact) / math.log(REL_MAX_DIST / max_exact)
                         * (REL_BUCKETS - max_exact)).astype(I32)
    large = jnp.minimum(large, REL_BUCKETS - 1)
    return jnp.where(rel < max_exact, rel, large)


def _bias_tiles(rel_bias):
    kb = KEY_BLOCK
    assert kb >= REL_MAX_DIST
    tq = jnp.arange(kb)[:, None]
    sk = jnp.arange(kb)[None, :]
    tiles = []
    for off in range(3):
        rel = jnp.maximum(off * kb + tq - sk, 0)
        bucket = _t5_bucket(rel)[..., None]
        tile = jnp.zeros((kb, kb, rel_bias.shape[1]), F32)
        for b in range(REL_BUCKETS):
            tile = jnp.where(bucket == b, rel_bias[b].astype(F32), tile)
        per_head = jnp.transpose(tile, (2, 1, 0))
        tiles.append(jnp.concatenate([per_head[0::2], per_head[1::2]], axis=-2))
    return jnp.stack(tiles, axis=0).astype(F32)


def _dsa(zq, zs, bias_tiles, bsz, seq_len):
    t = zq.shape[0]
    nq = seq_len // Q_ROWS
    nkb = seq_len // KEY_BLOCK
    k_sel = min(TOPK_MAX, seq_len // 4)
    aw = ATT_HEADS * ATT_DH
    npair = ATT_HEADS // 2
    return pl.pallas_call(
        functools.partial(_dsa_body, k_sel=k_sel),
        grid=(bsz, nq),
        in_specs=[
            pl.BlockSpec((Q_ROWS, zq.shape[1]), lambda b, i: (b * nq + i, 0)),
            pl.BlockSpec((Q_ROWS, zs.shape[1]), lambda b, i: (b * nq + i, 0)),
            pl.BlockSpec((seq_len, zs.shape[1]), lambda b, i: (b, 0)),
            pl.BlockSpec(bias_tiles.shape, lambda b, i: (0, 0, 0, 0)),
        ],
        out_specs=pl.BlockSpec((Q_ROWS, aw), lambda b, i: (b * nq + i, 0)),
        out_shape=jax.ShapeDtypeStruct((t, aw), BF16),
        scratch_shapes=[
            pltpu.VMEM((nkb, KEY_BLOCK, Q_ROWS), I32),
            pltpu.VMEM((nkb, KEY_BLOCK, Q_ROWS), I16),
            pltpu.VMEM((nkb, KEY_BLOCK, Q_ROWS), I16),
            pltpu.VMEM((nkb, KEY_BLOCK, Q_ROWS), F32),
            pltpu.VMEM((nkb, 2 * KEY_BLOCK, LANES), BF16),
            pltpu.VMEM((nkb, 2 * KEY_BLOCK, LANES), BF16),
            pltpu.VMEM((nkb, 2 * LANES, 2 * KEY_BLOCK), BF16),
            pltpu.VMEM((npair, 2 * LANES, Q_ROWS), F32),
        ],
        compiler_params=_params(("arbitrary", "arbitrary")),
        name="dsa",
    )(zq, zs, zs, bias_tiles)


def _merge_body(x_ref, oa_ref, ob_ref, zg_ref, gm_ref, shf_ref, scf_ref, nw_ref,
                wpa, wpb, wout, wrt, br, x1_ref, h2_ref, lgt_ref):
    d = x_ref.shape[1]
    ya = _dot(oa_ref[...], wpa[...])
    yb = _dot(ob_ref[...], wpb[...])
    merged = _sigmoid(zg_ref[:, :d]) * ya + _sigmoid(zg_ref[:, d:]) * yb
    y = _dot(merged.astype(BF16), wout[...])
    x1 = x_ref[...] + gm_ref[0] * y
    x1_ref[...] = x1
    h2 = _rms_mod(x1, nw_ref[...], scf_ref[0], shf_ref[0])
    for j in range(TOK_TILE):
        h2_ref[pl.ds(j, x_ref.shape[0], stride=TOK_TILE), :] = h2[:, j * LANES:(j + 1) * LANES]
    hh, hl = _split2(h2)
    wh, wl = _split2(wrt[...])
    lgt_ref[...] = _dot_nt(wh, hh) + (_dot_nt(wh, hl) + _dot_nt(wl, hh)) + br[...]


def _merge(x2d, oa, ob, zg, gate_m, shift_f, scale_f, nw, wpa, wpb, wout, wrt, br, seq_len):
    t, d = x2d.shape
    assert d == TOK_TILE * LANES
    tm = 256
    per_b = seq_len // tm
    e = wrt.shape[0]
    row = lambda i: (i, 0)
    const = lambda i: (0, 0)
    bmap = lambda i: (i // per_b, 0, 0)
    return pl.pallas_call(
        _merge_body,
        grid=(t // tm,),
        in_specs=[
            pl.BlockSpec((tm, d), row),
            pl.BlockSpec((tm, oa.shape[1]), row),
            pl.BlockSpec((tm, ob.shape[1]), row),
            pl.BlockSpec((tm, zg.shape[1]), row),
            pl.BlockSpec((1, 1, d), bmap),
            pl.BlockSpec((1, 1, d), bmap),
            pl.BlockSpec((1, 1, d), bmap),
            pl.BlockSpec((1, d), const),
            pl.BlockSpec(wpa.shape, const),
            pl.BlockSpec(wpb.shape, const),
            pl.BlockSpec(wout.shape, const),
            pl.BlockSpec(wrt.shape, const),
            pl.BlockSpec(br.shape, const),
        ],
        out_specs=[pl.BlockSpec((tm, d), row), pl.BlockSpec((tm * TOK_TILE, LANES), row),
                   pl.BlockSpec((e, tm), lambda i: (0, i))],
        out_shape=[jax.ShapeDtypeStruct((t, d), F32), jax.ShapeDtypeStruct((t * TOK_TILE, LANES), F32),
                   jax.ShapeDtypeStruct((e, t), F32)],
        compiler_params=_params(("parallel",)),
        name="merge",
    )(x2d, oa, ob, zg, gate_m, shift_f, scale_f, nw, wpa, wpb, wout, wrt, br)


def _route1_body(lg_ref, er_ref, gt_ref, cnt_ref, base_ref):
    i = pl.program_id(0)
    e, tr = lg_ref.shape

    @pl.when(i == 0)
    def _():
        base_ref[...] = jnp.zeros_like(base_ref)

    l = lg_ref[...]
    eidx = lax.broadcasted_iota(I32, (e, tr), 0).astype(F32)
    vals, es, ohs = [], [], []
    for _ in range(TOP_K):
        m = jnp.max(l, axis=0, keepdims=True)
        ek = jnp.min(jnp.where(l == m, eidx, float(e)), axis=0, keepdims=True)
        oh = eidx == ek
        l = jnp.where(oh, NEG_INF, l)
        vals.append(m)
        es.append(ek)
        ohs.append(oh)
    ps = [jnp.exp(v - vals[0]) for v in vals]
    den = ps[0] + ps[1] + ps[2] + ps[3]
    gs = [p / den for p in ps]

    member = jnp.zeros((e, tr), F32)
    for oh in ohs:
        member = member + jnp.where(oh, 1.0, 0.0)
    mb = member.astype(BF16)
    r2 = lax.broadcasted_iota(I32, (tr, tr), 0)
    c2 = lax.broadcasted_iota(I32, (tr, tr), 1)
    sut = jnp.where(r2 < c2, 1.0, 0.0).astype(BF16)
    base = base_ref[...]
    pref = _dot(mb, sut) + jnp.concatenate([base] * (tr // LANES), axis=1)
    ranks = [jnp.sum(jnp.where(oh, pref, 0.0), axis=0, keepdims=True) for oh in ohs]
    er_ref[...] = jnp.concatenate(es + ranks, axis=0).astype(I32)

    row8 = lax.broadcasted_iota(I32, (8, tr), 0)
    g8 = jnp.zeros((8, tr), F32)
    for kk in range(TOP_K):
        g8 = jnp.where(row8 == kk, gs[kk], g8)
    gfull = jnp.concatenate([g8, jnp.zeros((LANES - 8, tr), F32)], axis=0)
    gt_ref[...] = gfull.T

    new_base = base + _dot(mb, jnp.ones((tr, LANES), BF16))
    base_ref[...] = new_base
    cnt_ref[...] = new_base


def _route1(lgt):
    e, t = lgt.shape
    tr = 512
    return pl.pallas_call(
        _route1_body,
        grid=(t // tr,),
        in_specs=[pl.BlockSpec((e, tr), lambda i: (0, i))],
        out_specs=[pl.BlockSpec((2 * TOP_K, tr), lambda i: (0, i)),
                   pl.BlockSpec((tr, LANES), lambda i: (i, 0)),
                   pl.BlockSpec((e, LANES), lambda i: (0, 0))],
        out_shape=[jax.ShapeDtypeStruct((2 * TOP_K, t), I32),
                   jax.ShapeDtypeStruct((t, LANES), F32),
                   jax.ShapeDtypeStruct((e, LANES), F32)],
        scratch_shapes=[pltpu.VMEM((e, LANES), F32)],
        compiler_params=_params(("arbitrary",)),
        name="route_topk",
    )(lgt)


def _route2_body(er_ref, cnt_ref, slot_ref, be_ref, tot_ref, fs_ref):
    e = cnt_ref.shape[0]
    tt = er_ref.shape[1]
    nbp = be_ref.shape[1]
    sh = int(math.log2(MOE_BLOCK))
    cnt = cnt_ref[...].astype(I32)
    padded = ((cnt + (MOE_BLOCK - 1)) >> sh) << sh
    rows = []
    acc = jnp.zeros((1, LANES), I32)
    for ei in range(e):
        rows.append(acc)
        acc = acc + padded[ei:ei + 1, :]
    pstart = jnp.concatenate(rows, axis=0)
    pend = pstart + padded
    tot_ref[...] = jnp.broadcast_to(acc, tot_ref.shape)
    fs_ref[...] = jnp.where(padded > 0, pend - MOE_BLOCK, -1)

    er = er_ref[...]
    eidx = lax.broadcasted_iota(I32, (e, tt), 0)
    pst = jnp.concatenate([pstart] * (tt // LANES), axis=1)
    outs = []
    for kk in range(TOP_K):
        ps = jnp.sum(jnp.where(eidx == er[kk:kk + 1, :], pst, 0), axis=0, keepdims=True)
        outs.append(ps + er[TOP_K + kk:TOP_K + kk + 1, :])
    slot_ref[...] = jnp.concatenate(outs + [jnp.zeros((TOP_K, tt), I32)], axis=0)

    blk0 = lax.broadcasted_iota(I32, (e, nbp), 1) * MOE_BLOCK
    pe = jnp.concatenate([pend] * (nbp // LANES), axis=1)
    nle = jnp.sum(jnp.where(pe <= blk0, 1, 0), axis=0, keepdims=True)
    be_ref[...] = jnp.broadcast_to(jnp.minimum(nle, e - 1), be_ref.shape)


def _route2(er, cnt, nbp):
    t = er.shape[1]
    tt = min(2048, t)
    e = cnt.shape[0]
    return pl.pallas_call(
        _route2_body,
        grid=(t // tt,),
        in_specs=[pl.BlockSpec((2 * TOP_K, tt), lambda i: (0, i)),
                  pl.BlockSpec((e, LANES), lambda i: (0, 0))],
        out_specs=[pl.BlockSpec((2 * TOP_K, tt), lambda i: (0, i)),
                   pl.BlockSpec((8, nbp), lambda i: (0, 0)),
                   pl.BlockSpec((8, LANES), lambda i: (0, 0)),
                   pl.BlockSpec((e, LANES), lambda i: (0, 0))],
        out_shape=[jax.ShapeDtypeStruct((2 * TOP_K, t), I32),
                   jax.ShapeDtypeStruct((8, nbp), I32),
                   jax.ShapeDtypeStruct((8, LANES), I32),
                   jax.ShapeDtypeStruct((e, LANES), I32)],
        compiler_params=_params(("arbitrary",)),
        name="route_slots",
    )(er, cnt)


def _tile(row):
    return pl.ds(pl.multiple_of(row * TOK_TILE, TOK_TILE), TOK_TILE)


def _dispatch_body(fs_ref, nu_ref, slot_ref, h_ref, xs_ref, zbuf, sem, zsem):
    i = pl.program_id(0)
    td = slot_ref.shape[1]

    @pl.when(i == 0)
    def _():
        zbuf[...] = jnp.zeros_like(zbuf)
        zrows = zbuf.shape[0]
        nblk = xs_ref.shape[0] // zrows
        fills = []
        for e in range(fs_ref.shape[0]):
            fills.append((fs_ref[e] >= 0, jnp.maximum(fs_ref[e], 0) * TOK_TILE))
        for b in range(fs_ref.shape[0]):
            blk = nu_ref[0] + b
            fills.append((blk < nblk, jnp.minimum(blk, nblk - 1) * zrows))
        for pred, start in fills:
            @pl.when(pred)
            def _(start=start):
                pltpu.make_async_copy(zbuf, xs_ref.at[pl.ds(pl.multiple_of(start, zrows), zrows)], zsem).start()
        for pred, _ in fills:
            @pl.when(pred)
            def _():
                pltpu.make_async_copy(zbuf, xs_ref.at[pl.ds(0, zrows)], zsem).wait()

    def issue(g, c):
        for u in range(DMA_UNROLL):
            tok = g * DMA_UNROLL + u
            for kk in range(TOP_K):
                pltpu.make_async_copy(h_ref.at[_tile(tok)], xs_ref.at[_tile(slot_ref[kk, tok])], sem).start(
                    priority=(u * TOP_K + kk) % 2)
        return c

    lax.fori_loop(0, td // DMA_UNROLL, issue, 0)
    for kk in range(TOP_K):
        pltpu.make_async_copy(h_ref, xs_ref.at[pl.ds(0, td * TOK_TILE)], sem).wait()


def _dispatch(fill_start, nused, slot, h2, n_slots):
    t = h2.shape[0] // TOK_TILE
    td = 512
    grid_spec = pltpu.PrefetchScalarGridSpec(
        num_scalar_prefetch=2,
        grid=(t // td,),
        in_specs=[pl.BlockSpec((2 * TOP_K, td), lambda i, fs, nu: (0, i), memory_space=pltpu.SMEM),
                  pl.BlockSpec((td * TOK_TILE, LANES), lambda i, fs, nu: (i, 0))],
        out_specs=pl.BlockSpec(memory_space=pl.ANY),
        scratch_shapes=[pltpu.VMEM((MOE_BLOCK * TOK_TILE, LANES), h2.dtype), pltpu.SemaphoreType.DMA, pltpu.SemaphoreType.DMA],
    )
    return pl.pallas_call(
        _dispatch_body,
        grid_spec=grid_spec,
        out_shape=jax.ShapeDtypeStruct((n_slots * TOK_TILE, LANES), h2.dtype),
        compiler_params=_params(("arbitrary",)),
        name="moe_dispatch",
    )(fill_start, nused, slot, h2)


def _expert_body(be_ref, nu_ref, xs_ref, w1_ref, b1_ref, w2_ref, b2_ref, y_ref, w1b, w2b):
    i = pl.program_id(0)
    f = w2_ref.shape[2]

    @pl.when(jnp.logical_or(i == 0, be_ref[i] != be_ref[jnp.maximum(i - 1, 0)]))
    def _():
        w1b[...] = w1_ref[0, 0].astype(BF16)
        w2b[...] = w2_ref[0, 0].astype(BF16)

    @pl.when(i < nu_ref[0])
    def _():
        x = jnp.concatenate([xs_ref[pl.ds(j, MOE_BLOCK, stride=TOK_TILE), :] for j in range(TOK_TILE)], axis=1)
        hm = _dot(x.astype(BF16), w1b[...]) + b1_ref[0, 0]
        glu = jnp.minimum(hm[:, :f], SWIGLU_LIMIT)
        lin = jnp.clip(hm[:, f:], -SWIGLU_LIMIT, SWIGLU_LIMIT)
        act = glu * _sigmoid(SWIGLU_ALPHA * glu) * (lin + 1.0)
        y = _dot(act.astype(BF16), w2b[...]) + b2_ref[0, 0]
        for j in range(TOK_TILE):
            y_ref[pl.ds(j, MOE_BLOCK, stride=TOK_TILE), :] = y[:, j * LANES:(j + 1) * LANES]

    @pl.when(i >= nu_ref[0])
    def _():
        y_ref[...] = jnp.zeros_like(y_ref)


def _experts(be, nused, xs, w1, b1, w2, b2, layer):
    ns = xs.shape[0] // TOK_TILE
    depth, e, d, f2 = w1.shape
    f = w2.shape[2]
    nblk = ns // MOE_BLOCK
    used = lambda i, nu: jnp.minimum(i, nu[0] - 1)
    grid_spec = pltpu.PrefetchScalarGridSpec(
        num_scalar_prefetch=2,
        grid=(nblk,),
        in_specs=[
            pl.BlockSpec((MOE_BLOCK * TOK_TILE, LANES), lambda i, be, nu: (used(i, nu), 0)),
            pl.BlockSpec((1, 1, d, f2), lambda i, be, nu: (layer, be[i], 0, 0)),
            pl.BlockSpec((1, 1, 1, f2), lambda i, be, nu: (layer, be[i], 0, 0)),
            pl.BlockSpec((1, 1, f, d), lambda i, be, nu: (layer, be[i], 0, 0)),
            pl.BlockSpec((1, 1, 1, d), lambda i, be, nu: (layer, be[i], 0, 0)),
        ],
        out_specs=pl.BlockSpec((MOE_BLOCK * TOK_TILE, LANES), lambda i, be, nu: (i, 0)),
        scratch_shapes=[pltpu.VMEM((d, f2), BF16), pltpu.VMEM((f, d), BF16)],
    )
    return pl.pallas_call(
        _expert_body,
        grid_spec=grid_spec,
        out_shape=jax.ShapeDtypeStruct((ns * TOK_TILE, LANES), F32),
        compiler_params=_params(("arbitrary",)),
        name="moe_experts",
    )(be, nused, xs, w1, b1.reshape(depth, e, 1, f2), w2, b2.reshape(depth, e, 1, d))


def _combine_body(slot_ref, y_ref, x1_ref, gt_ref, gf_ref, fw_ref, o_ref, buf, sem, *, final):
    td = x1_ref.shape[0]

    def issue(g, c):
        for u in range(DMA_UNROLL):
            tok = g * DMA_UNROLL + u
            for kk in range(TOP_K):
                pltpu.make_async_copy(y_ref.at[_tile(slot_ref[kk, tok])], buf.at[_tile(kk * td + tok)], sem).start(
                    priority=(u * TOP_K + kk) % 2)
        return c

    lax.fori_loop(0, td // DMA_UNROLL, issue, 0)
    for kk in range(TOP_K):
        pltpu.make_async_copy(y_ref.at[pl.ds(0, td * TOK_TILE)], buf.at[pl.ds(kk * td * TOK_TILE, td * TOK_TILE)], sem).wait()
    g = gt_ref[...]
    gk = [g[:, kk:kk + 1] for kk in range(TOP_K)]
    parts = []
    for j in range(TOK_TILE):
        y = gk[0] * buf[pl.ds(j, td, stride=TOK_TILE), :]
        for kk in range(1, TOP_K):
            y = y + gk[kk] * buf[pl.ds(kk * td * TOK_TILE + j, td, stride=TOK_TILE), :]
        parts.append(x1_ref[:, j * LANES:(j + 1) * LANES] + gf_ref[0][:, j * LANES:(j + 1) * LANES] * y)
    x2 = jnp.concatenate(parts, axis=-1)
    if final:
        ms = jnp.mean(x2 * x2, axis=-1, keepdims=True)
        x2 = x2 * lax.rsqrt(ms + NORM_EPS) * fw_ref[...]
    o_ref[...] = x2


def _combine(slot, ys, x1, gt, gate_f, fw, seq_len, final):
    t, d = x1.shape
    td = 256
    per_b = seq_len // td
    return pl.pallas_call(
        functools.partial(_combine_body, final=final),
        grid=(t // td,),
        in_specs=[pl.BlockSpec((2 * TOP_K, td), lambda i: (0, i), memory_space=pltpu.SMEM),
                  pl.BlockSpec(memory_space=pl.ANY),
                  pl.BlockSpec((td, d), lambda i: (i, 0)),
                  pl.BlockSpec((td, LANES), lambda i: (i, 0)),
                  pl.BlockSpec((1, 1, d), lambda i: (i // per_b, 0, 0)),
                  pl.BlockSpec((1, d), lambda i: (0, 0))],
        out_specs=pl.BlockSpec((td, d), lambda i: (i, 0)),
        out_shape=jax.ShapeDtypeStruct((t, d), F32),
        scratch_shapes=[pltpu.VMEM((TOP_K * td * TOK_TILE, LANES), F32), pltpu.SemaphoreType.DMA],
        compiler_params=_params(("arbitrary",)),
        name="moe_combine",
    )(slot, ys, x1, gt, gate_f, fw)


def _split_w_in(w):
    hw = HGRN_HEADS * HGRN_DK
    aw = ATT_HEADS * ATT_DH
    iw = IDX_HEADS * IDX_DH
    d = w.shape[0]
    o = 4 * hw
    qb = w[:, o:o + aw]
    kb = w[:, o + aw:o + aw + ATT_DH]
    vb = w[:, o + aw + ATT_DH:o + aw + 2 * ATT_DH]
    o2 = o + aw + 2 * ATT_DH
    iq = w[:, o2:o2 + iw]
    ik = w[:, o2 + iw:o2 + iw + IDX_DH]
    ih = w[:, o2 + iw + IDX_DH:o2 + iw + IDX_DH + IDX_HEADS]
    o3 = o2 + iw + IDX_DH + IDX_HEADS
    small = jnp.concatenate([kb, vb, ik, ih, jnp.zeros((d, 2 * LANES - 3 * ATT_DH - IDX_HEADS), w.dtype)], axis=1)
    groups = (w[:, :o], jnp.concatenate([qb, iq], axis=1), small, w[:, o3:])
    return tuple(g.astype(BF16) for g in groups)


def kernel(x, c, w_ada, b_ada, norm_mix_w, w_in, hgrn_lb_logits, hgrn_norm_w, rel_bias, w_proj_a, w_proj_b,
           w_out, norm_ffn_w, w_router, b_router, w_mlp1, b_mlp1, w_mlp2, b_mlp2, final_norm_w):
    bsz, seq_len, d = x.shape
    depth = w_in.shape[0]
    n_exp = w_router.shape[2]
    t = bsz * seq_len
    n_slots = t * TOP_K + n_exp * MOE_BLOCK
    nblk = n_slots // MOE_BLOCK
    nbp = -(-nblk // LANES) * LANES

    mod = _ada(c, w_ada, b_ada)
    bias_tiles = _bias_tiles(rel_bias)
    x2d = x.reshape(t, d)
    for l in range(depth):
        m6 = mod[l].reshape(bsz, N_MOD, 1, d)
        shift_m, scale_m, gate_m, shift_f, scale_f, gate_f = (m6[:, n] for n in range(N_MOD))
        zh, zq, zs, zg = _inproj(x2d, shift_m, scale_m, norm_mix_w[l].reshape(1, d), _split_w_in(w_in[l]), seq_len)
        oa = _hgrn(zh, hgrn_lb_logits, hgrn_norm_w[l], l, bsz, seq_len).reshape(t, -1)
        ob = _dsa(zq, zs, bias_tiles, bsz, seq_len)
        x1, h2, lgt = _merge(x2d, oa, ob, zg, gate_m, shift_f, scale_f, norm_ffn_w[l].reshape(1, d),
                             w_proj_a[l].astype(BF16), w_proj_b[l].astype(BF16), w_out[l].astype(BF16),
                             w_router[l].T, b_router[l].reshape(n_exp, 1), seq_len)
        er, gt, cnt = _route1(lgt)
        slot, be, tot, fill_start = _route2(er, cnt, nbp)
        nused = (tot[0, :1] >> int(math.log2(MOE_BLOCK))).astype(I32)
        xs = _dispatch(fill_start[:, 0], nused, slot, h2, n_slots)
        ys = _experts(be[0, :nblk], nused, xs, w_mlp1, b_mlp1, w_mlp2, b_mlp2, l)
        x2d = _combine(slot, ys, x1, gt, gate_f, final_norm_w.reshape(1, d), seq_len, final=(l == depth - 1))
    return x2d.reshape(bsz, seq_len, d)
```

```python
import functools
import math

import jax
import jax.numpy as jnp
from jax import lax
from jax.experimental import pallas as pl
from jax.experimental.pallas import tpu as pltpu

F32 = jnp.float32
BF16 = jnp.bfloat16
I32 = jnp.int32
I16 = jnp.int16

HGRN_HEADS = 4
HGRN_DK = 128
ATT_HEADS = 8
ATT_DH = 64
IDX_HEADS = 8
IDX_DH = 64
TOPK_MAX = 256
REL_BUCKETS = 32
REL_MAX_DIST = 128
TOP_K = 4
SWIGLU_ALPHA = 1.702
SWIGLU_LIMIT = 7.0
NORM_EPS = 1e-6
N_MOD = 6

LANES = 128
HGRN_CHUNK = 128
HGRN_SUB = 4
HGRN_PAIR = 4
KEY_BLOCK = 128
Q_ROWS = 256
MOE_BLOCK = 512
DMA_UNROLL = 8
TOK_TILE = 8
VMEM_LIMIT = 56 * 1024 * 1024
NEG_INF = float("-inf")
INT_MIN = -(2 ** 31)


def _dot(a, b):
    return jnp.dot(a, b, preferred_element_type=F32)


def _dot_nt(a, b):
    return lax.dot_general(a, b, (((1,), (1,)), ((), ())), preferred_element_type=F32)


def _dot_tn(a, b):
    return lax.dot_general(a, b, (((0,), (0,)), ((), ())), preferred_element_type=F32)


def _split2(a):
    hi = a.astype(BF16)
    lo = (a - hi.astype(F32)).astype(BF16)
    return hi, lo


def _split3(a):
    hi = a.astype(BF16)
    r = a - hi.astype(F32)
    mid = r.astype(BF16)
    lo = (r - mid.astype(F32)).astype(BF16)
    return hi, mid, lo


def _sigmoid(x):
    return 0.5 * jnp.tanh(0.5 * x) + 0.5


def _params(sem):
    return pltpu.CompilerParams(dimension_semantics=sem, vmem_limit_bytes=VMEM_LIMIT)


def _ada_body(c_ref, w_ref, b_ref, o_ref):
    c = c_ref[...]
    ca = c * _sigmoid(c)
    o_ref[0] = _dot(ca, w_ref[0]) + b_ref[0]


def _ada(c, w_ada, b_ada):
    depth, d, nd = w_ada.shape
    bsz = c.shape[0]
    tn = 1024
    return pl.pallas_call(
        _ada_body,
        grid=(depth, nd // tn),
        in_specs=[
            pl.BlockSpec((bsz, d), lambda l, j: (0, 0)),
            pl.BlockSpec((1, d, tn), lambda l, j: (l, 0, j)),
            pl.BlockSpec((1, 1, tn), lambda l, j: (l, 0, j)),
        ],
        out_specs=pl.BlockSpec((1, bsz, tn), lambda l, j: (l, 0, j)),
        out_shape=jax.ShapeDtypeStruct((depth, bsz, nd), F32),
        compiler_params=_params(("parallel", "parallel")),
        name="ada_mod",
    )(c, w_ada, b_ada.reshape(depth, 1, nd))


def _rms_mod(x, nw, scale, shift):
    ms = jnp.mean(x * x, axis=-1, keepdims=True)
    y = x * lax.rsqrt(ms + NORM_EPS) * nw
    return y * (1.0 + scale) + shift


def _inproj_body(x_ref, sh_ref, sc_ref, nw_ref, w1, w2, w3, w4, zh, zq, zs, zg):
    h = _rms_mod(x_ref[...], nw_ref[...], sc_ref[0], sh_ref[0]).astype(BF16)
    zh[...] = _dot(h, w1[...])
    zq[...] = _dot(h, w2[...]).astype(BF16)
    zs[...] = _dot(h, w3[...])
    zg[...] = _dot(h, w4[...])


def _inproj(x2d, shift, scale, nw, ws, seq_len):
    t, d = x2d.shape
    tm = 256
    per_b = seq_len // tm
    w1, w2, w3, w4 = ws
    row = lambda i: (i, 0)
    const = lambda i: (0, 0)
    bmap = lambda i: (i // per_b, 0, 0)
    outs = [
        jax.ShapeDtypeStruct((t, w1.shape[1]), F32),
        jax.ShapeDtypeStruct((t, w2.shape[1]), BF16),
        jax.ShapeDtypeStruct((t, w3.shape[1]), F32),
        jax.ShapeDtypeStruct((t, w4.shape[1]), F32),
    ]
    return pl.pallas_call(
        _inproj_body,
        grid=(t // tm,),
        in_specs=[
            pl.BlockSpec((tm, d), row),
            pl.BlockSpec((1, 1, d), bmap),
            pl.BlockSpec((1, 1, d), bmap),
            pl.BlockSpec((1, d), const),
            pl.BlockSpec(w1.shape, const),
            pl.BlockSpec(w2.shape, const),
            pl.BlockSpec(w3.shape, const),
            pl.BlockSpec(w4.shape, const),
        ],
        out_specs=[pl.BlockSpec((tm, o.shape[1]), row) for o in outs],
        out_shape=outs,
        compiler_params=_params(("parallel",)),
        name="in_proj",
    )(x2d, shift, scale, nw, w1, w2, w3, w4)


def _hgrn_body(lbl_ref, qa_ref, fa_ref, ia_ref, ga_ref, nw_ref, o_ref, *, layer, seq_len):
    C, R = HGRN_CHUNK, HGRN_SUB
    nh = qa_ref.shape[2] // LANES
    lg = lbl_ref[...]
    ex = jnp.exp(lg - jnp.max(lg, axis=0, keepdims=True))
    sm = ex / jnp.sum(ex, axis=0, keepdims=True)
    lb_all = jnp.zeros((1, nh * LANES), F32)
    for li in range(1, layer + 1):
        lb_all = lb_all + sm[li]
    nw_all = nw_ref[...]

    rowi = lax.broadcasted_iota(I32, (C, LANES), 0)
    ri2 = lax.broadcasted_iota(I32, (C, C), 0)
    ci2 = lax.broadcasted_iota(I32, (C, C), 1)
    ltri = jnp.where(ci2 <= ri2, 1.0, 0.0).astype(BF16)
    ones_bf = jnp.ones((LANES, LANES), BF16)
    dlt = ri2 - ci2

    def one_head(qa, z, v, ga, st, lb, nw):
        log_lb = jnp.log(lb)
        log_1m = jnp.log1p(-lb)
        q = qa * _sigmoid(qa)
        ls = jnp.minimum(z, 0.0) - jnp.log1p(jnp.exp(-jnp.abs(z)))
        t2 = log_1m + ls
        mx = jnp.maximum(log_lb, t2)
        g = mx + jnp.log(jnp.exp(log_lb - mx) + jnp.exp(t2 - mx))
        k = (1.0 - lb) * _sigmoid(-z)
        g1, g2, g3 = _split3(g)
        b = _dot(ltri, g1) + (_dot(ltri, g2) + _dot(ltri, g3))

        a = jnp.zeros((C, C), F32)
        half = C // 2
        while half >= R:
            span = 2 * half
            nb = C // span
            refb = jnp.concatenate(
                [jnp.broadcast_to(b[p * span + half - 1:p * span + half, :], (span, LANES)) for p in range(nb)],
                axis=0)
            second = (rowi & (span - 1)) >= half
            qt = q * jnp.exp(jnp.where(second, b - refb, NEG_INF))
            kt = k * jnp.exp(jnp.where(second, NEG_INF, refb - b))
            al = _dot_nt(qt.astype(BF16), kt.astype(BF16))
            sh = int(math.log2(span))
            a = a + jnp.where((ri2 >> sh) == (ci2 >> sh), al, 0.0)
            half //= 2

        ps = []
        for dl in range(R):
            if dl == 0:
                kd, bd = k, b
            else:
                kd = pltpu.roll(k, dl, 0)
                bd = pltpu.roll(b, dl, 0)
            arg = jnp.where((rowi & (R - 1)) >= dl, b - bd, NEG_INF)
            ps.append((q * kd * jnp.exp(arg)).astype(BF16))
        rs = _dot(jnp.concatenate(ps, axis=0), ones_bf)
        ad = jnp.zeros((C, C), F32)
        for dl in range(R):
            ad = jnp.where(dlt == dl, rs[dl * C:(dl + 1) * C, :], ad)
        shr = int(math.log2(R))
        a = a + jnp.where((ri2 >> shr) == (ci2 >> shr), ad, 0.0)

        vb = v.astype(BF16)
        o = _dot(a.astype(BF16), vb) + _dot_nt((q * jnp.exp(b)).astype(BF16), st.astype(BF16))
        bl = b[C - 1:C, :]
        kdec = (k * jnp.exp(bl - b)).astype(BF16)
        st_new = st * jnp.exp(bl) + _dot_tn(vb, kdec)
        on = o * lax.rsqrt(jnp.mean(o * o, axis=-1, keepdims=True) + NORM_EPS) * nw
        return (on * (ga * _sigmoid(ga))).astype(o_ref.dtype), st_new

    def chunk(ci, sts):
        rows = pl.ds(pl.multiple_of(ci * C, C), C)
        new = []
        for hd in range(nh):
            ln = slice(hd * LANES, (hd + 1) * LANES)
            out, st_new = one_head(qa_ref[0, rows, ln], fa_ref[0, rows, ln], ia_ref[0, rows, ln], ga_ref[0, rows, ln],
                                   sts[hd], lb_all[:, ln], nw_all[:, ln])
            o_ref[0, rows, ln] = out
            new.append(st_new)
        return tuple(new)

    lax.fori_loop(0, seq_len // C, chunk, tuple(jnp.zeros((LANES, LANES), F32) for _ in range(nh)))


def _hgrn(zh, lb_logits, norm_w, layer, bsz, seq_len):
    depth = lb_logits.shape[0]
    hw = HGRN_HEADS * HGRN_DK
    wb = HGRN_PAIR * LANES
    ng = HGRN_HEADS // HGRN_PAIR
    z3 = zh.reshape(bsz, seq_len, zh.shape[1])
    blk = lambda off: pl.BlockSpec((1, seq_len, wb), lambda b, h: (b, 0, off + h))
    return pl.pallas_call(
        functools.partial(_hgrn_body, layer=layer, seq_len=seq_len),
        grid=(bsz, ng),
        in_specs=[
            pl.BlockSpec((depth, 1, wb), lambda b, h: (0, 0, h)),
            blk(0), blk(ng), blk(2 * ng), blk(3 * ng),
            pl.BlockSpec((1, wb), lambda b, h: (0, h)),
        ],
        out_specs=pl.BlockSpec((1, seq_len, wb), lambda b, h: (b, 0, h)),
        out_shape=jax.ShapeDtypeStruct((bsz, seq_len, hw), BF16),
        compiler_params=_params(("parallel", "parallel")),
        name="hgrn2",
    )(lb_logits.reshape(depth, 1, hw), z3, z3, z3, z3, norm_w.reshape(1, hw))


def _float_key(x):
    x = jnp.where(x == 0.0, 0.0, x)
    bits = lax.bitcast_convert_type(x, I32)
    return bits ^ ((bits >> 31) & 0x7FFFFFFF)


def _dsa_body(zq_ref, zsq_ref, zs_ref, bias_ref, o_ref,
              key_ref, hi_ref, lo_ref, sel_ref, k2_ref, ik2_ref, v2t_ref, acc_ref, *, k_sel):
    i = pl.program_id(1)
    tq, kb = Q_ROWS, KEY_BLOCK
    sub = tq // kb
    nkb = key_ref.shape[0]
    aw = ATT_HEADS * ATT_DH
    npair = ATT_HEADS // 2
    half = LANES // 2
    nb = sub * (i + 1)

    @pl.when(i == 0)
    def _():
        lo = lax.broadcasted_iota(I32, (kb, LANES), 1) < half
        one = jnp.where(lo, 1.0, 0.0)

        def build(j, c):
            r = pl.multiple_of(j * kb, kb)
            kv = zs_ref[pl.ds(r, kb), 0:LANES]
            ix = zs_ref[pl.ds(r, kb), LANES:2 * LANES]
            vk = pltpu.roll(kv, half, 1)
            xi = pltpu.roll(ix, half, 1)
            sc = ATT_DH ** -0.5
            k2_ref[j] = jnp.concatenate([jnp.where(lo, kv * sc, 0.0), jnp.where(lo, 0.0, vk * sc)], axis=0).astype(BF16)
            ik2_ref[j] = jnp.concatenate([jnp.where(lo, ix, 0.0), jnp.where(lo, 0.0, xi)], axis=0).astype(BF16)
            top = jnp.concatenate([jnp.where(lo, vk, 0.0), one], axis=1)
            bot = jnp.concatenate([jnp.where(lo, 0.0, kv), 1.0 - one], axis=1)
            v2t_ref[j] = jnp.concatenate([top, bot], axis=0).T.astype(BF16)
            return c

        lax.fori_loop(0, nkb, build, 0)

    krow = lax.broadcasted_iota(I32, (kb, tq), 0)
    qcol = lax.broadcasted_iota(I32, (kb, tq), 1)
    r1 = lax.broadcasted_iota(I32, (LANES, LANES), 0)
    c1 = lax.broadcasted_iota(I32, (LANES, LANES), 1)
    slt = jnp.where(c1 < r1, 1.0, 0.0).astype(BF16)
    iwt = (zsq_ref[:, LANES:2 * LANES] * (IDX_HEADS ** -0.5 * IDX_DH ** -0.5)).T

    def causal(j):
        return (kb * j + krow) <= (tq * i + qcol)

    def score_one(j):
        acc = jnp.zeros((kb, tq), F32)
        for p in range(IDX_HEADS // 2):
            s = _dot_nt(ik2_ref[j], zq_ref[:, aw + p * LANES:aw + (p + 1) * LANES])
            w0 = iwt[half + 2 * p:half + 2 * p + 1, :]
            w1 = iwt[half + 2 * p + 1:half + 2 * p + 2, :]
            acc = acc + (w0 * jnp.maximum(s[:kb], 0.0) + w1 * jnp.maximum(s[kb:], 0.0))
        key = _float_key(jnp.where(causal(j), acc, NEG_INF))
        key_ref[j] = key
        hi_ref[j] = (key >> 16).astype(I16)

    def score_grp(g, c):
        for u in range(sub):
            score_one(sub * g + u)
        return c

    lax.fori_loop(0, i + 1, score_grp, 0)

    kf = float(k_sel)

    def count(pred):
        def grp(g, acc):
            for u in range(sub):
                hit = jnp.where(pred(key_ref[sub * g + u]), 1.0, 0.0)
                acc = acc + jnp.sum(hit.reshape(kb // 8, 8, tq), axis=0)
            return acc
        acc = lax.fori_loop(0, i + 1, grp, jnp.zeros((8, tq), F32))
        return jnp.sum(acc, axis=0, keepdims=True)

    def count16(ref, pred):
        def grp(g, acc):
            for u in range(sub):
                hit = jnp.where(pred(ref[sub * g + u]), jnp.int16(1), jnp.int16(0))
                for r in range(kb // 16):
                    acc = acc + hit[16 * r:16 * (r + 1), :]
            return acc
        acc = lax.fori_loop(0, i + 1, grp, jnp.zeros((16, tq), I16))
        return jnp.sum(acc.astype(F32), axis=0, keepdims=True)

    def kth16(ref, base):
        zero16 = jnp.zeros((1, tq), I16)
        t0 = jnp.where(base + count16(ref, lambda v: v >= zero16) >= kf, 0, -32768).astype(I32)

        def step(it, t):
            cand = t | jnp.left_shift(1, 14 - it)
            c16 = cand.astype(I16)
            return jnp.where(base + count16(ref, lambda v: v >= c16) >= kf, cand, t)

        return lax.fori_loop(0, 15, step, t0)

    t_hi = kth16(hi_ref, 0.0)
    h16 = t_hi.astype(I16)
    above = count16(hi_ref, lambda v: v > h16)

    def low_blk(j, c):
        lo = ((key_ref[j] & 0xFFFF) - 32768).astype(I16)
        lo_ref[j] = jnp.where(hi_ref[j] == h16, lo, jnp.int16(-32768))
        return c

    lax.fori_loop(0, nb, low_blk, 0)
    t_lo = kth16(lo_ref, above)
    thr = (t_hi << 16) | ((t_lo + 32768) & 0xFFFF)
    need = kf - count(lambda kj: kj > thr)

    excess = jnp.max(count(lambda kj: kj == thr) - need) > 0.5

    @pl.when(excess)
    def _():
        def sel_blk(j, carry):
            kj = key_ref[j]
            eq = jnp.where(kj == thr, 1.0, 0.0)
            pref = _dot(slt, eq.astype(BF16)) + carry
            tie_ok = jnp.where(kj == thr, jnp.where(pref < need, 1.0, 0.0), 0.0)
            chosen = jnp.where(kj > thr, 1.0, tie_ok)
            chosen = jnp.where(causal(j), chosen, 0.0)
            sel_ref[j] = jnp.where(chosen > 0.0, 0.0, NEG_INF)
            return carry + jnp.sum(eq, axis=0, keepdims=True)

        lax.fori_loop(0, nb, sel_blk, jnp.zeros((1, tq), F32))

    @pl.when(jnp.logical_not(excess))
    def _():
        def sel_blk(j, c):
            sel_ref[j] = jnp.where(key_ref[j] >= thr, jnp.where(causal(j), 0.0, NEG_INF), NEG_INF)
            return c

        lax.fori_loop(0, nb, sel_blk, 0)

    def logits(j, p):
        bias = jnp.concatenate([bias_ref[jnp.clip(sub * i + u - j, 0, 2), p] for u in range(sub)], axis=1)
        sel = sel_ref[j]
        s = _dot_nt(k2_ref[j], zq_ref[:, p * LANES:(p + 1) * LANES])
        return s + bias + jnp.concatenate([sel, sel], axis=0)

    def col_max(x):
        return jnp.max(jnp.max(x.reshape(kb // 8, 8, tq), axis=0), axis=0, keepdims=True)

    acc_ref[...] = jnp.zeros(acc_ref.shape, F32)
    quarter = kb // 2

    def att_grp(g, ms):
        out = []
        for p in range(npair):
            ss = [logits(sub * g + u, p) for u in range(sub)]
            pes = [[None, None] for _ in range(sub)]
            als, mn = [], []
            for hd in range(2):
                sl = [x[hd * kb:(hd + 1) * kb] for x in ss]
                m_old = ms[2 * p + hd]
                m_new = m_old
                for x in sl:
                    m_new = jnp.maximum(m_new, col_max(x))
                m_safe = jnp.where(m_new == NEG_INF, 0.0, m_new)
                for u in range(sub):
                    pes[u][hd] = jnp.exp(sl[u] - m_safe)
                als.append(jnp.broadcast_to(jnp.exp(m_old - m_safe), (quarter, tq)))
                mn.append(m_new)
            upd = acc_ref[p] * jnp.concatenate(als + als, axis=0)
            for u in range(sub):
                upd = upd + _dot(v2t_ref[sub * g + u], jnp.concatenate(pes[u], axis=0).astype(BF16))
            acc_ref[p] = upd
            out += mn
        return tuple(out)

    lax.fori_loop(0, i + 1, att_grp, tuple(jnp.full((1, tq), NEG_INF, F32) for _ in range(ATT_HEADS)))
    for p in range(npair):
        a = acc_ref[p]
        o_ref[:, p * LANES:(p + 1) * LANES] = (a[:kb] / a[kb:]).T.astype(o_ref.dtype)


def _t5_bucket(rel):
    max_exact = REL_BUCKETS // 2
    rel_f = jnp.maximum(rel, 1).astype(F32)
    large = max_exact + (jnp.log(rel_f / max_exact) / math.log(REL_MAX_DIST / max_exact)
                         * (REL_BUCKETS - max_exact)).astype(I32)
    large = jnp.minimum(large, REL_BUCKETS - 1)
    return jnp.where(rel < max_exact, rel, large)


def _bias_tiles(rel_bias):
    kb = KEY_BLOCK
    assert kb >= REL_MAX_DIST
    tq = jnp.arange(kb)[:, None]
    sk = jnp.arange(kb)[None, :]
    tiles = []
    for off in range(3):
        rel = jnp.maximum(off * kb + tq - sk, 0)
        bucket = _t5_bucket(rel)[..., None]
        tile = jnp.zeros((kb, kb, rel_bias.shape[1]), F32)
        for b in range(REL_BUCKETS):
            tile = jnp.where(bucket == b, rel_bias[b].astype(F32), tile)
        per_head = jnp.transpose(tile, (2, 1, 0))
        tiles.append(jnp.concatenate([per_head[0::2], per_head[1::2]], axis=-2))
    return jnp.stack(tiles, axis=0).astype(F32)


def _dsa(zq, zs, bias_tiles, bsz, seq_len):
    t = zq.shape[0]
    nq = seq_len // Q_ROWS
    nkb = seq_len // KEY_BLOCK
    k_sel = min(TOPK_MAX, seq_len // 4)
    aw = ATT_HEADS * ATT_DH
    npair = ATT_HEADS // 2
    return pl.pallas_call(
        functools.partial(_dsa_body, k_sel=k_sel),
        grid=(bsz, nq),
        in_specs=[
            pl.BlockSpec((Q_ROWS, zq.shape[1]), lambda b, i: (b * nq + i, 0)),
            pl.BlockSpec((Q_ROWS, zs.shape[1]), lambda b, i: (b * nq + i, 0)),
            pl.BlockSpec((seq_len, zs.shape[1]), lambda b, i: (b, 0)),
            pl.BlockSpec(bias_tiles.shape, lambda b, i: (0, 0, 0, 0)),
        ],
        out_specs=pl.BlockSpec((Q_ROWS, aw), lambda b, i: (b * nq + i, 0)),
        out_shape=jax.ShapeDtypeStruct((t, aw), BF16),
        scratch_shapes=[
            pltpu.VMEM((nkb, KEY_BLOCK, Q_ROWS), I32),
            pltpu.VMEM((nkb, KEY_BLOCK, Q_ROWS), I16),
            pltpu.VMEM((nkb, KEY_BLOCK, Q_ROWS), I16),
            pltpu.VMEM((nkb, KEY_BLOCK, Q_ROWS), F32),
            pltpu.VMEM((nkb, 2 * KEY_BLOCK, LANES), BF16),
            pltpu.VMEM((nkb, 2 * KEY_BLOCK, LANES), BF16),
            pltpu.VMEM((nkb, 2 * LANES, 2 * KEY_BLOCK), BF16),
            pltpu.VMEM((npair, 2 * LANES, Q_ROWS), F32),
        ],
        compiler_params=_params(("arbitrary", "arbitrary")),
        name="dsa",
    )(zq, zs, zs, bias_tiles)


def _merge_body(x_ref, oa_ref, ob_ref, zg_ref, gm_ref, shf_ref, scf_ref, nw_ref,
                wpa, wpb, wout, wrt, br, x1_ref, h2_ref, lgt_ref):
    d = x_ref.shape[1]
    ya = _dot(oa_ref[...], wpa[...])
    yb = _dot(ob_ref[...], wpb[...])
    merged = _sigmoid(zg_ref[:, :d]) * ya + _sigmoid(zg_ref[:, d:]) * yb
    y = _dot(merged.astype(BF16), wout[...])
    x1 = x_ref[...] + gm_ref[0] * y
    x1_ref[...] = x1
    h2 = _rms_mod(x1, nw_ref[...], scf_ref[0], shf_ref[0])
    for j in range(TOK_TILE):
        h2_ref[pl.ds(j, x_ref.shape[0], stride=TOK_TILE), :] = h2[:, j * LANES:(j + 1) * LANES]
    hh, hl = _split2(h2)
    wh, wl = _split2(wrt[...])
    lgt_ref[...] = _dot_nt(wh, hh) + (_dot_nt(wh, hl) + _dot_nt(wl, hh)) + br[...]


def _merge(x2d, oa, ob, zg, gate_m, shift_f, scale_f, nw, wpa, wpb, wout, wrt, br, seq_len):
    t, d = x2d.shape
    assert d == TOK_TILE * LANES
    tm = 256
    per_b = seq_len // tm
    e = wrt.shape[0]
    row = lambda i: (i, 0)
    const = lambda i: (0, 0)
    bmap = lambda i: (i // per_b, 0, 0)
    return pl.pallas_call(
        _merge_body,
        grid=(t // tm,),
        in_specs=[
            pl.BlockSpec((tm, d), row),
            pl.BlockSpec((tm, oa.shape[1]), row),
            pl.BlockSpec((tm, ob.shape[1]), row),
            pl.BlockSpec((tm, zg.shape[1]), row),
            pl.BlockSpec((1, 1, d), bmap),
            pl.BlockSpec((1, 1, d), bmap),
            pl.BlockSpec((1, 1, d), bmap),
            pl.BlockSpec((1, d), const),
            pl.BlockSpec(wpa.shape, const),
            pl.BlockSpec(wpb.shape, const),
            pl.BlockSpec(wout.shape, const),
            pl.BlockSpec(wrt.shape, const),
            pl.BlockSpec(br.shape, const),
        ],
        out_specs=[pl.BlockSpec((tm, d), row), pl.BlockSpec((tm * TOK_TILE, LANES), row),
                   pl.BlockSpec((e, tm), lambda i: (0, i))],
        out_shape=[jax.ShapeDtypeStruct((t, d), F32), jax.ShapeDtypeStruct((t * TOK_TILE, LANES), F32),
                   jax.ShapeDtypeStruct((e, t), F32)],
        compiler_params=_params(("parallel",)),
        name="merge",
    )(x2d, oa, ob, zg, gate_m, shift_f, scale_f, nw, wpa, wpb, wout, wrt, br)


def _route1_body(lg_ref, er_ref, gt_ref, cnt_ref, base_ref):
    i = pl.program_id(0)
    e, tr = lg_ref.shape

    @pl.when(i == 0)
    def _():
        base_ref[...] = jnp.zeros_like(base_ref)

    l = lg_ref[...]
    eidx = lax.broadcasted_iota(I32, (e, tr), 0).astype(F32)
    vals, es, ohs = [], [], []
    for _ in range(TOP_K):
        m = jnp.max(l, axis=0, keepdims=True)
        ek = jnp.min(jnp.where(l == m, eidx, float(e)), axis=0, keepdims=True)
        oh = eidx == ek
        l = jnp.where(oh, NEG_INF, l)
        vals.append(m)
        es.append(ek)
        ohs.append(oh)
    ps = [jnp.exp(v - vals[0]) for v in vals]
    den = ps[0] + ps[1] + ps[2] + ps[3]
    gs = [p / den for p in ps]

    member = jnp.zeros((e, tr), F32)
    for oh in ohs:
        member = member + jnp.where(oh, 1.0, 0.0)
    mb = member.astype(BF16)
    r2 = lax.broadcasted_iota(I32, (tr, tr), 0)
    c2 = lax.broadcasted_iota(I32, (tr, tr), 1)
    sut = jnp.where(r2 < c2, 1.0, 0.0).astype(BF16)
    base = base_ref[...]
    pref = _dot(mb, sut) + jnp.concatenate([base] * (tr // LANES), axis=1)
    ranks = [jnp.sum(jnp.where(oh, pref, 0.0), axis=0, keepdims=True) for oh in ohs]
    er_ref[...] = jnp.concatenate(es + ranks, axis=0).astype(I32)

    row8 = lax.broadcasted_iota(I32, (8, tr), 0)
    g8 = jnp.zeros((8, tr), F32)
    for kk in range(TOP_K):
        g8 = jnp.where(row8 == kk, gs[kk], g8)
    gfull = jnp.concatenate([g8, jnp.zeros((LANES - 8, tr), F32)], axis=0)
    gt_ref[...] = gfull.T

    new_base = base + _dot(mb, jnp.ones((tr, LANES), BF16))
    base_ref[...] = new_base
    cnt_ref[...] = new_base


def _route1(lgt):
    e, t = lgt.shape
    tr = 512
    return pl.pallas_call(
        _route1_body,
        grid=(t // tr,),
        in_specs=[pl.BlockSpec((e, tr), lambda i: (0, i))],
        out_specs=[pl.BlockSpec((2 * TOP_K, tr), lambda i: (0, i)),
                   pl.BlockSpec((tr, LANES), lambda i: (i, 0)),
                   pl.BlockSpec((e, LANES), lambda i: (0, 0))],
        out_shape=[jax.ShapeDtypeStruct((2 * TOP_K, t), I32),
                   jax.ShapeDtypeStruct((t, LANES), F32),
                   jax.ShapeDtypeStruct((e, LANES), F32)],
        scratch_shapes=[pltpu.VMEM((e, LANES), F32)],
        compiler_params=_params(("arbitrary",)),
        name="route_topk",
    )(lgt)


def _route2_body(er_ref, cnt_ref, slot_ref, be_ref, tot_ref, fs_ref):
    e = cnt_ref.shape[0]
    tt = er_ref.shape[1]
    nbp = be_ref.shape[1]
    sh = int(math.log2(MOE_BLOCK))
    cnt = cnt_ref[...].astype(I32)
    padded = ((cnt + (MOE_BLOCK - 1)) >> sh) << sh
    rows = []
    acc = jnp.zeros((1, LANES), I32)
    for ei in range(e):
        rows.append(acc)
        acc = acc + padded[ei:ei + 1, :]
    pstart = jnp.concatenate(rows, axis=0)
    pend = pstart + padded
    tot_ref[...] = jnp.broadcast_to(acc, tot_ref.shape)
    fs_ref[...] = jnp.where(padded > 0, pend - MOE_BLOCK, -1)

    er = er_ref[...]
    eidx = lax.broadcasted_iota(I32, (e, tt), 0)
    pst = jnp.concatenate([pstart] * (tt // LANES), axis=1)
    outs = []
    for kk in range(TOP_K):
        ps = jnp.sum(jnp.where(eidx == er[kk:kk + 1, :], pst, 0), axis=0, keepdims=True)
        outs.append(ps + er[TOP_K + kk:TOP_K + kk + 1, :])
    slot_ref[...] = jnp.concatenate(outs + [jnp.zeros((TOP_K, tt), I32)], axis=0)

    blk0 = lax.broadcasted_iota(I32, (e, nbp), 1) * MOE_BLOCK
    pe = jnp.concatenate([pend] * (nbp // LANES), axis=1)
    nle = jnp.sum(jnp.where(pe <= blk0, 1, 0), axis=0, keepdims=True)
    be_ref[...] = jnp.broadcast_to(jnp.minimum(nle, e - 1), be_ref.shape)


def _route2(er, cnt, nbp):
    t = er.shape[1]
    tt = min(2048, t)
    e = cnt.shape[0]
    return pl.pallas_call(
        _route2_body,
        grid=(t // tt,),
        in_specs=[pl.BlockSpec((2 * TOP_K, tt), lambda i: (0, i)),
                  pl.BlockSpec((e, LANES), lambda i: (0, 0))],
        out_specs=[pl.BlockSpec((2 * TOP_K, tt), lambda i: (0, i)),
                   pl.BlockSpec((8, nbp), lambda i: (0, 0)),
                   pl.BlockSpec((8, LANES), lambda i: (0, 0)),
                   pl.BlockSpec((e, LANES), lambda i: (0, 0))],
        out_shape=[jax.ShapeDtypeStruct((2 * TOP_K, t), I32),
                   jax.ShapeDtypeStruct((8, nbp), I32),
                   jax.ShapeDtypeStruct((8, LANES), I32),
                   jax.ShapeDtypeStruct((e, LANES), I32)],
        compiler_params=_params(("arbitrary",)),
        name="route_slots",
    )(er, cnt)


def _tile(row):
    return pl.ds(pl.multiple_of(row * TOK_TILE, TOK_TILE), TOK_TILE)


def _dispatch_body(fs_ref, nu_ref, slot_ref, h_ref, xs_ref, zbuf, sem, zsem):
    i = pl.program_id(0)
    td = slot_ref.shape[1]

    @pl.when(i == 0)
    def _():
        zbuf[...] = jnp.zeros_like(zbuf)
        zrows = zbuf.shape[0]
        nblk = xs_ref.shape[0] // zrows
        fills = []
        for e in range(fs_ref.shape[0]):
            fills.append((fs_ref[e] >= 0, jnp.maximum(fs_ref[e], 0) * TOK_TILE))
        for b in range(fs_ref.shape[0]):
            blk = nu_ref[0] + b
            fills.append((blk < nblk, jnp.minimum(blk, nblk - 1) * zrows))
        for pred, start in fills:
            @pl.when(pred)
            def _(start=start):
                pltpu.make_async_copy(zbuf, xs_ref.at[pl.ds(pl.multiple_of(start, zrows), zrows)], zsem).start()
        for pred, _ in fills:
            @pl.when(pred)
            def _():
                pltpu.make_async_copy(zbuf, xs_ref.at[pl.ds(0, zrows)], zsem).wait()

    def issue(g, c):
        for u in range(DMA_UNROLL):
            tok = g * DMA_UNROLL + u
            for kk in range(TOP_K):
                pltpu.make_async_copy(h_ref.at[_tile(tok)], xs_ref.at[_tile(slot_ref[kk, tok])], sem).start(
                    priority=(u * TOP_K + kk) % 2)
        return c

    lax.fori_loop(0, td // DMA_UNROLL, issue, 0)
    for kk in range(TOP_K):
        pltpu.make_async_copy(h_ref, xs_ref.at[pl.ds(0, td * TOK_TILE)], sem).wait()


def _dispatch(fill_start, nused, slot, h2, n_slots):
    t = h2.shape[0] // TOK_TILE
    td = 512
    grid_spec = pltpu.PrefetchScalarGridSpec(
        num_scalar_prefetch=2,
        grid=(t // td,),
        in_specs=[pl.BlockSpec((2 * TOP_K, td), lambda i, fs, nu: (0, i), memory_space=pltpu.SMEM),
                  pl.BlockSpec((td * TOK_TILE, LANES), lambda i, fs, nu: (i, 0))],
        out_specs=pl.BlockSpec(memory_space=pl.ANY),
        scratch_shapes=[pltpu.VMEM((MOE_BLOCK * TOK_TILE, LANES), h2.dtype), pltpu.SemaphoreType.DMA, pltpu.SemaphoreType.DMA],
    )
    return pl.pallas_call(
        _dispatch_body,
        grid_spec=grid_spec,
        out_shape=jax.ShapeDtypeStruct((n_slots * TOK_TILE, LANES), h2.dtype),
        compiler_params=_params(("arbitrary",)),
        name="moe_dispatch",
    )(fill_start, nused, slot, h2)


def _expert_body(be_ref, nu_ref, xs_ref, w1_ref, b1_ref, w2_ref, b2_ref, y_ref, w1b, w2b):
    i = pl.program_id(0)
    f = w2_ref.shape[2]

    @pl.when(jnp.logical_or(i == 0, be_ref[i] != be_ref[jnp.maximum(i - 1, 0)]))
    def _():
        w1b[...] = w1_ref[0, 0].astype(BF16)
        w2b[...] = w2_ref[0, 0].astype(BF16)

    @pl.when(i < nu_ref[0])
    def _():
        x = jnp.concatenate([xs_ref[pl.ds(j, MOE_BLOCK, stride=TOK_TILE), :] for j in range(TOK_TILE)], axis=1)
        hm = _dot(x.astype(BF16), w1b[...]) + b1_ref[0, 0]
        glu = jnp.minimum(hm[:, :f], SWIGLU_LIMIT)
        lin = jnp.clip(hm[:, f:], -SWIGLU_LIMIT, SWIGLU_LIMIT)
        act = glu * _sigmoid(SWIGLU_ALPHA * glu) * (lin + 1.0)
        y = _dot(act.astype(BF16), w2b[...]) + b2_ref[0, 0]
        for j in range(TOK_TILE):
            y_ref[pl.ds(j, MOE_BLOCK, stride=TOK_TILE), :] = y[:, j * LANES:(j + 1) * LANES]

    @pl.when(i >= nu_ref[0])
    def _():
        y_ref[...] = jnp.zeros_like(y_ref)


def _experts(be, nused, xs, w1, b1, w2, b2, layer):
    ns = xs.shape[0] // TOK_TILE
    depth, e, d, f2 = w1.shape
    f = w2.shape[2]
    nblk = ns // MOE_BLOCK
    used = lambda i, nu: jnp.minimum(i, nu[0] - 1)
    grid_spec = pltpu.PrefetchScalarGridSpec(
        num_scalar_prefetch=2,
        grid=(nblk,),
        in_specs=[
            pl.BlockSpec((MOE_BLOCK * TOK_TILE, LANES), lambda i, be, nu: (used(i, nu), 0)),
            pl.BlockSpec((1, 1, d, f2), lambda i, be, nu: (layer, be[i], 0, 0)),
            pl.BlockSpec((1, 1, 1, f2), lambda i, be, nu: (layer, be[i], 0, 0)),
            pl.BlockSpec((1, 1, f, d), lambda i, be, nu: (layer, be[i], 0, 0)),
            pl.BlockSpec((1, 1, 1, d), lambda i, be, nu: (layer, be[i], 0, 0)),
        ],
        out_specs=pl.BlockSpec((MOE_BLOCK * TOK_TILE, LANES), lambda i, be, nu: (i, 0)),
        scratch_shapes=[pltpu.VMEM((d, f2), BF16), pltpu.VMEM((f, d), BF16)],
    )
    return pl.pallas_call(
        _expert_body,
        grid_spec=grid_spec,
        out_shape=jax.ShapeDtypeStruct((ns * TOK_TILE, LANES), F32),
        compiler_params=_params(("arbitrary",)),
        name="moe_experts",
    )(be, nused, xs, w1, b1.reshape(depth, e, 1, f2), w2, b2.reshape(depth, e, 1, d))


def _combine_body(slot_ref, y_ref, x1_ref, gt_ref, gf_ref, fw_ref, o_ref, buf, sem, *, final):
    td = x1_ref.shape[0]

    def issue(g, c):
        for u in range(DMA_UNROLL):
            tok = g * DMA_UNROLL + u
            for kk in range(TOP_K):
                pltpu.make_async_copy(y_ref.at[_tile(slot_ref[kk, tok])], buf.at[_tile(kk * td + tok)], sem).start(
                    priority=(u * TOP_K + kk) % 2)
        return c

    lax.fori_loop(0, td // DMA_UNROLL, issue, 0)
    for kk in range(TOP_K):
        pltpu.make_async_copy(y_ref.at[pl.ds(0, td * TOK_TILE)], buf.at[pl.ds(kk * td * TOK_TILE, td * TOK_TILE)], sem).wait()
    g = gt_ref[...]
    gk = [g[:, kk:kk + 1] for kk in range(TOP_K)]
    parts = []
    for j in range(TOK_TILE):
        y = gk[0] * buf[pl.ds(j, td, stride=TOK_TILE), :]
        for kk in range(1, TOP_K):
            y = y + gk[kk] * buf[pl.ds(kk * td * TOK_TILE + j, td, stride=TOK_TILE), :]
        parts.append(x1_ref[:, j * LANES:(j + 1) * LANES] + gf_ref[0][:, j * LANES:(j + 1) * LANES] * y)
    x2 = jnp.concatenate(parts, axis=-1)
    if final:
        ms = jnp.mean(x2 * x2, axis=-1, keepdims=True)
        x2 = x2 * lax.rsqrt(ms + NORM_EPS) * fw_ref[...]
    o_ref[...] = x2


def _combine(slot, ys, x1, gt, gate_f, fw, seq_len, final):
    t, d = x1.shape
    td = 256
    per_b = seq_len // td
    return pl.pallas_call(
        functools.partial(_combine_body, final=final),
        grid=(t // td,),
        in_specs=[pl.BlockSpec((2 * TOP_K, td), lambda i: (0, i), memory_space=pltpu.SMEM),
                  pl.BlockSpec(memory_space=pl.ANY),
                  pl.BlockSpec((td, d), lambda i: (i, 0)),
                  pl.BlockSpec((td, LANES), lambda i: (i, 0)),
                  pl.BlockSpec((1, 1, d), lambda i: (i // per_b, 0, 0)),
                  pl.BlockSpec((1, d), lambda i: (0, 0))],
        out_specs=pl.BlockSpec((td, d), lambda i: (i, 0)),
        out_shape=jax.ShapeDtypeStruct((t, d), F32),
        scratch_shapes=[pltpu.VMEM((TOP_K * td * TOK_TILE, LANES), F32), pltpu.SemaphoreType.DMA],
        compiler_params=_params(("arbitrary",)),
        name="moe_combine",
    )(slot, ys, x1, gt, gate_f, fw)


def _split_w_in(w):
    hw = HGRN_HEADS * HGRN_DK
    aw = ATT_HEADS * ATT_DH
    iw = IDX_HEADS * IDX_DH
    d = w.shape[0]
    o = 4 * hw
    qb = w[:, o:o + aw]
    kb = w[:, o + aw:o + aw + ATT_DH]
    vb = w[:, o + aw + ATT_DH:o + aw + 2 * ATT_DH]
    o2 = o + aw + 2 * ATT_DH
    iq = w[:, o2:o2 + iw]
    ik = w[:, o2 + iw:o2 + iw + IDX_DH]
    ih = w[:, o2 + iw + IDX_DH:o2 + iw + IDX_DH + IDX_HEADS]
    o3 = o2 + iw + IDX_DH + IDX_HEADS
    small = jnp.concatenate([kb, vb, ik, ih, jnp.zeros((d, 2 * LANES - 3 * ATT_DH - IDX_HEADS), w.dtype)], axis=1)
    groups = (w[:, :o], jnp.concatenate([qb, iq], axis=1), small, w[:, o3:])
    return tuple(g.astype(BF16) for g in groups)


def kernel(x, c, w_ada, b_ada, norm_mix_w, w_in, hgrn_lb_logits, hgrn_norm_w, rel_bias, w_proj_a, w_proj_b,
           w_out, norm_ffn_w, w_router, b_router, w_mlp1, b_mlp1, w_mlp2, b_mlp2, final_norm_w):
    bsz, seq_len, d = x.shape
    depth = w_in.shape[0]
    n_exp = w_router.shape[2]
    t = bsz * seq_len
    n_slots = t * TOP_K + n_exp * MOE_BLOCK
    nblk = n_slots // MOE_BLOCK
    nbp = -(-nblk // LANES) * LANES

    mod = _ada(c, w_ada, b_ada)
    bias_tiles = _bias_tiles(rel_bias)
    x2d = x.reshape(t, d)
    for l in range(depth):
        m6 = mod[l].reshape(bsz, N_MOD, 1, d)
        shift_m, scale_m, gate_m, shift_f, scale_f, gate_f = (m6[:, n] for n in range(N_MOD))
        zh, zq, zs, zg = _inproj(x2d, shift_m, scale_m, norm_mix_w[l].reshape(1, d), _split_w_in(w_in[l]), seq_len)
        oa = _hgrn(zh, hgrn_lb_logits, hgrn_norm_w[l], l, bsz, seq_len).reshape(t, -1)
        ob = _dsa(zq, zs, bias_tiles, bsz, seq_len)
        x1, h2, lgt = _merge(x2d, oa, ob, zg, gate_m, shift_f, scale_f, norm_ffn_w[l].reshape(1, d),
                             w_proj_a[l].astype(BF16), w_proj_b[l].astype(BF16), w_out[l].astype(BF16),
                             w_router[l].T, b_router[l].reshape(n_exp, 1), seq_len)
        er, gt, cnt = _route1(lgt)
        slot, be, tot, fill_start = _route2(er, cnt, nbp)
        nused = (tot[0, :1] >> int(math.log2(MOE_BLOCK))).astype(I32)
        xs = _dispatch(fill_start[:, 0], nused, slot, h2, n_slots)
        ys = _experts(be[0, :nblk], nused, xs, w_mlp1, b_mlp1, w_mlp2, b_mlp2, l)
        x2d = _combine(slot, ys, x1, gt, gate_f, final_norm_w.reshape(1, d), seq_len, final=(l == depth - 1))
    return x2d.reshape(bsz, seq_len, d)
```

```python
import functools
import math

import jax
import jax.numpy as jnp
from jax import lax
from jax.experimental import pallas as pl
from jax.experimental.pallas import tpu as pltpu

F32 = jnp.float32
BF16 = jnp.bfloat16
I32 = jnp.int32
I16 = jnp.int16

HGRN_HEADS = 4
HGRN_DK = 128
ATT_HEADS = 8
ATT_DH = 64
IDX_HEADS = 8
IDX_DH = 64
TOPK_MAX = 256
REL_BUCKETS = 32
REL_MAX_DIST = 128
TOP_K = 4
SWIGLU_ALPHA = 1.702
SWIGLU_LIMIT = 7.0
NORM_EPS = 1e-6
N_MOD = 6

LANES = 128
HGRN_CHUNK = 128
HGRN_SUB = 4
HGRN_PAIR = 4
KEY_BLOCK = 128
Q_ROWS = 256
MOE_BLOCK = 512
DMA_UNROLL = 8
TOK_TILE = 8
VMEM_LIMIT = 56 * 1024 * 1024
NEG_INF = float("-inf")
INT_MIN = -(2 ** 31)


def _dot(a, b):
    return jnp.dot(a, b, preferred_element_type=F32)


def _dot_nt(a, b):
    return lax.dot_general(a, b, (((1,), (1,)), ((), ())), preferred_element_type=F32)


def _dot_tn(a, b):
    return lax.dot_general(a, b, (((0,), (0,)), ((), ())), preferred_element_type=F32)


def _split2(a):
    hi = a.astype(BF16)
    lo = (a - hi.astype(F32)).astype(BF16)
    return hi, lo


def _split3(a):
    hi = a.astype(BF16)
    r = a - hi.astype(F32)
    mid = r.astype(BF16)
    lo = (r - mid.astype(F32)).astype(BF16)
    return hi, mid, lo


def _sigmoid(x):
    return 0.5 * jnp.tanh(0.5 * x) + 0.5


def _params(sem):
    return pltpu.CompilerParams(dimension_semantics=sem, vmem_limit_bytes=VMEM_LIMIT)


def _ada_body(c_ref, w_ref, b_ref, o_ref):
    c = c_ref[...]
    ca = c * _sigmoid(c)
    o_ref[0] = _dot(ca, w_ref[0]) + b_ref[0]


def _ada(c, w_ada, b_ada):
    depth, d, nd = w_ada.shape
    bsz = c.shape[0]
    tn = 1024
    return pl.pallas_call(
        _ada_body,
        grid=(depth, nd // tn),
        in_specs=[
            pl.BlockSpec((bsz, d), lambda l, j: (0, 0)),
            pl.BlockSpec((1, d, tn), lambda l, j: (l, 0, j)),
            pl.BlockSpec((1, 1, tn), lambda l, j: (l, 0, j)),
        ],
        out_specs=pl.BlockSpec((1, bsz, tn), lambda l, j: (l, 0, j)),
        out_shape=jax.ShapeDtypeStruct((depth, bsz, nd), F32),
        compiler_params=_params(("parallel", "parallel")),
        name="ada_mod",
    )(c, w_ada, b_ada.reshape(depth, 1, nd))


def _rms_mod(x, nw, scale, shift):
    ms = jnp.mean(x * x, axis=-1, keepdims=True)
    y = x * lax.rsqrt(ms + NORM_EPS) * nw
    return y * (1.0 + scale) + shift


def _inproj_body(x_ref, sh_ref, sc_ref, nw_ref, w1, w2, w3, w4, zh, zq, zs, zg):
    h = _rms_mod(x_ref[...], nw_ref[...], sc_ref[0], sh_ref[0]).astype(BF16)
    zh[...] = _dot(h, w1[...])
    zq[...] = _dot(h, w2[...]).astype(BF16)
    zs[...] = _dot(h, w3[...])
    zg[...] = _dot(h, w4[...])


def _inproj(x2d, shift, scale, nw, ws, seq_len):
    t, d = x2d.shape
    tm = 256
    per_b = seq_len // tm
    w1, w2, w3, w4 = ws
    row = lambda i: (i, 0)
    const = lambda i: (0, 0)
    bmap = lambda i: (i // per_b, 0, 0)
    outs = [
        jax.ShapeDtypeStruct((t, w1.shape[1]), F32),
        jax.ShapeDtypeStruct((t, w2.shape[1]), BF16),
        jax.ShapeDtypeStruct((t, w3.shape[1]), F32),
        jax.ShapeDtypeStruct((t, w4.shape[1]), F32),
    ]
    return pl.pallas_call(
        _inproj_body,
        grid=(t // tm,),
        in_specs=[
            pl.BlockSpec((tm, d), row),
            pl.BlockSpec((1, 1, d), bmap),
            pl.BlockSpec((1, 1, d), bmap),
            pl.BlockSpec((1, d), const),
            pl.BlockSpec(w1.shape, const),
            pl.BlockSpec(w2.shape, const),
            pl.BlockSpec(w3.shape, const),
            pl.BlockSpec(w4.shape, const),
        ],
        out_specs=[pl.BlockSpec((tm, o.shape[1]), row) for o in outs],
        out_shape=outs,
        compiler_params=_params(("parallel",)),
        name="in_proj",
    )(x2d, shift, scale, nw, w1, w2, w3, w4)


def _hgrn_body(lbl_ref, qa_ref, fa_ref, ia_ref, ga_ref, nw_ref, o_ref, *, layer, seq_len):
    C, R = HGRN_CHUNK, HGRN_SUB
    nh = qa_ref.shape[2] // LANES
    lg = lbl_ref[...]
    ex = jnp.exp(lg - jnp.max(lg, axis=0, keepdims=True))
    sm = ex / jnp.sum(ex, axis=0, keepdims=True)
    lb_all = jnp.zeros((1, nh * LANES), F32)
    for li in range(1, layer + 1):
        lb_all = lb_all + sm[li]
    nw_all = nw_ref[...]

    rowi = lax.broadcasted_iota(I32, (C, LANES), 0)
    ri2 = lax.broadcasted_iota(I32, (C, C), 0)
    ci2 = lax.broadcasted_iota(I32, (C, C), 1)
    ltri = jnp.where(ci2 <= ri2, 1.0, 0.0).astype(BF16)
    ones_bf = jnp.ones((LANES, LANES), BF16)
    dlt = ri2 - ci2

    def one_head(qa, z, v, ga, st, lb, nw):
        log_lb = jnp.log(lb)
        log_1m = jnp.log1p(-lb)
        q = qa * _sigmoid(qa)
        ls = jnp.minimum(z, 0.0) - jnp.log1p(jnp.exp(-jnp.abs(z)))
        t2 = log_1m + ls
        mx = jnp.maximum(log_lb, t2)
        g = mx + jnp.log(jnp.exp(log_lb - mx) + jnp.exp(t2 - mx))
        k = (1.0 - lb) * _sigmoid(-z)
        g1, g2, g3 = _split3(g)
        b = _dot(ltri, g1) + (_dot(ltri, g2) + _dot(ltri, g3))

        a = jnp.zeros((C, C), F32)
        half = C // 2
        while half >= R:
            span = 2 * half
            nb = C // span
            refb = jnp.concatenate(
                [jnp.broadcast_to(b[p * span + half - 1:p * span + half, :], (span, LANES)) for p in range(nb)],
                axis=0)
            second = (rowi & (span - 1)) >= half
            qt = q * jnp.exp(jnp.where(second, b - refb, NEG_INF))
            kt = k * jnp.exp(jnp.where(second, NEG_INF, refb - b))
            al = _dot_nt(qt.astype(BF16), kt.astype(BF16))
            sh = int(math.log2(span))
            a = a + jnp.where((ri2 >> sh) == (ci2 >> sh), al, 0.0)
            half //= 2

        ps = []
        for dl in range(R):
            if dl == 0:
                kd, bd = k, b
            else:
                kd = pltpu.roll(k, dl, 0)
                bd = pltpu.roll(b, dl, 0)
            arg = jnp.where((rowi & (R - 1)) >= dl, b - bd, NEG_INF)
            ps.append((q * kd * jnp.exp(arg)).astype(BF16))
        rs = _dot(jnp.concatenate(ps, axis=0), ones_bf)
        ad = jnp.zeros((C, C), F32)
        for dl in range(R):
            ad = jnp.where(dlt == dl, rs[dl * C:(dl + 1) * C, :], ad)
        shr = int(math.log2(R))
        a = a + jnp.where((ri2 >> shr) == (ci2 >> shr), ad, 0.0)

        vb = v.astype(BF16)
        o = _dot(a.astype(BF16), vb) + _dot_nt((q * jnp.exp(b)).astype(BF16), st.astype(BF16))
        bl = b[C - 1:C, :]
        kdec = (k * jnp.exp(bl - b)).astype(BF16)
        st_new = st * jnp.exp(bl) + _dot_tn(vb, kdec)
        on = o * lax.rsqrt(jnp.mean(o * o, axis=-1, keepdims=True) + NORM_EPS) * nw
        return (on * (ga * _sigmoid(ga))).astype(o_ref.dtype), st_new

    def chunk(ci, sts):
        rows = pl.ds(pl.multiple_of(ci * C, C), C)
        new = []
        for hd in range(nh):
            ln = slice(hd * LANES, (hd + 1) * LANES)
            out, st_new = one_head(qa_ref[0, rows, ln], fa_ref[0, rows, ln], ia_ref[0, rows, ln], ga_ref[0, rows, ln],
                                   sts[hd], lb_all[:, ln], nw_all[:, ln])
            o_ref[0, rows, ln] = out
            new.append(st_new)
        return tuple(new)

    lax.fori_loop(0, seq_len // C, chunk, tuple(jnp.zeros((LANES, LANES), F32) for _ in range(nh)))


def _hgrn(zh, lb_logits, norm_w, layer, bsz, seq_len):
    depth = lb_logits.shape[0]
    hw = HGRN_HEADS * HGRN_DK
    wb = HGRN_PAIR * LANES
    ng = HGRN_HEADS // HGRN_PAIR
    z3 = zh.reshape(bsz, seq_len, zh.shape[1])
    blk = lambda off: pl.BlockSpec((1, seq_len, wb), lambda b, h: (b, 0, off + h))
    return pl.pallas_call(
        functools.partial(_hgrn_body, layer=layer, seq_len=seq_len),
        grid=(bsz, ng),
        in_specs=[
            pl.BlockSpec((depth, 1, wb), lambda b, h: (0, 0, h)),
            blk(0), blk(ng), blk(2 * ng), blk(3 * ng),
            pl.BlockSpec((1, wb), lambda b, h: (0, h)),
        ],
        out_specs=pl.BlockSpec((1, seq_len, wb), lambda b, h: (b, 0, h)),
        out_shape=jax.ShapeDtypeStruct((bsz, seq_len, hw), BF16),
        compiler_params=_params(("parallel", "parallel")),
        name="hgrn2",
    )(lb_logits.reshape(depth, 1, hw), z3, z3, z3, z3, norm_w.reshape(1, hw))


def _float_key(x):
    x = jnp.where(x == 0.0, 0.0, x)
    bits = lax.bitcast_convert_type(x, I32)
    return bits ^ ((bits >> 31) & 0x7FFFFFFF)


def _dsa_body(zq_ref, zsq_ref, zs_ref, bias_ref, o_ref,
              key_ref, hi_ref, lo_ref, sel_ref, k2_ref, ik2_ref, v2t_ref, acc_ref, *, k_sel):
    i = pl.program_id(1)
    tq, kb = Q_ROWS, KEY_BLOCK
    sub = tq // kb
    nkb = key_ref.shape[0]
    aw = ATT_HEADS * ATT_DH
    npair = ATT_HEADS // 2
    half = LANES // 2
    nb = sub * (i + 1)

    @pl.when(i == 0)
    def _():
        lo = lax.broadcasted_iota(I32, (kb, LANES), 1) < half
        one = jnp.where(lo, 1.0, 0.0)

        def build(j, c):
            r = pl.multiple_of(j * kb, kb)
            kv = zs_ref[pl.ds(r, kb), 0:LANES]
            ix = zs_ref[pl.ds(r, kb), LANES:2 * LANES]
            vk = pltpu.roll(kv, half, 1)
            xi = pltpu.roll(ix, half, 1)
            sc = ATT_DH ** -0.5
            k2_ref[j] = jnp.concatenate([jnp.where(lo, kv * sc, 0.0), jnp.where(lo, 0.0, vk * sc)], axis=0).astype(BF16)
            ik2_ref[j] = jnp.concatenate([jnp.where(lo, ix, 0.0), jnp.where(lo, 0.0, xi)], axis=0).astype(BF16)
            top = jnp.concatenate([jnp.where(lo, vk, 0.0), one], axis=1)
            bot = jnp.concatenate([jnp.where(lo, 0.0, kv), 1.0 - one], axis=1)
            v2t_ref[j] = jnp.concatenate([top, bot], axis=0).T.astype(BF16)
            return c

        lax.fori_loop(0, nkb, build, 0)

    krow = lax.broadcasted_iota(I32, (kb, tq), 0)
    qcol = lax.broadcasted_iota(I32, (kb, tq), 1)
    r1 = lax.broadcasted_iota(I32, (LANES, LANES), 0)
    c1 = lax.broadcasted_iota(I32, (LANES, LANES), 1)
    slt = jnp.where(c1 < r1, 1.0, 0.0).astype(BF16)
    iwt = (zsq_ref[:, LANES:2 * LANES] * (IDX_HEADS ** -0.5 * IDX_DH ** -0.5)).T

    def causal(j):
        return (kb * j + krow) <= (tq * i + qcol)

    def score_one(j):
        acc = jnp.zeros((kb, tq), F32)
        for p in range(IDX_HEADS // 2):
            s = _dot_nt(ik2_ref[j], zq_ref[:, aw + p * LANES:aw + (p + 1) * LANES])
            w0 = iwt[half + 2 * p:half + 2 * p + 1, :]
            w1 = iwt[half + 2 * p + 1:half + 2 * p + 2, :]
            acc = acc + (w0 * jnp.maximum(s[:kb], 0.0) + w1 * jnp.maximum(s[kb:], 0.0))
        key = _float_key(jnp.where(causal(j), acc, NEG_INF))
        key_ref[j] = key
        hi_ref[j] = (key >> 16).astype(I16)

    def score_grp(g, c):
        for u in range(sub):
            score_one(sub * g + u)
        return c

    lax.fori_loop(0, i + 1, score_grp, 0)

    kf = float(k_sel)

    def count(pred):
        def grp(g, acc):
            for u in range(sub):
                hit = jnp.where(pred(key_ref[sub * g + u]), 1.0, 0.0)
                acc = acc + jnp.sum(hit.reshape(kb // 8, 8, tq), axis=0)
            return acc
        acc = lax.fori_loop(0, i + 1, grp, jnp.zeros((8, tq), F32))
        return jnp.sum(acc, axis=0, keepdims=True)

    def count16(ref, pred):
        def grp(g, acc):
            for u in range(sub):
                hit = jnp.where(pred(ref[sub * g + u]), jnp.int16(1), jnp.int16(0))
                for r in range(kb // 16):
                    acc = acc + hit[16 * r:16 * (r + 1), :]
            return acc
        acc = lax.fori_loop(0, i + 1, grp, jnp.zeros((16, tq), I16))
        return jnp.sum(acc.astype(F32), axis=0, keepdims=True)

    def kth16(ref, base):
        zero16 = jnp.zeros((1, tq), I16)
        t0 = jnp.where(base + count16(ref, lambda v: v >= zero16) >= kf, 0, -32768).astype(I32)

        def step(it, t):
            cand = t | jnp.left_shift(1, 14 - it)
            c16 = cand.astype(I16)
            return jnp.where(base + count16(ref, lambda v: v >= c16) >= kf, cand, t)

        return lax.fori_loop(0, 15, step, t0)

    t_hi = kth16(hi_ref, 0.0)
    h16 = t_hi.astype(I16)
    above = count16(hi_ref, lambda v: v > h16)

    def low_blk(j, c):
        lo = ((key_ref[j] & 0xFFFF) - 32768).astype(I16)
        lo_ref[j] = jnp.where(hi_ref[j] == h16, lo, jnp.int16(-32768))
        return c

    lax.fori_loop(0, nb, low_blk, 0)
    t_lo = kth16(lo_ref, above)
    thr = (t_hi << 16) | ((t_lo + 32768) & 0xFFFF)
    need = kf - count(lambda kj: kj > thr)

    excess = jnp.max(count(lambda kj: kj == thr) - need) > 0.5

    @pl.when(excess)
    def _():
        def sel_blk(j, carry):
            kj = key_ref[j]
            eq = jnp.where(kj == thr, 1.0, 0.0)
            pref = _dot(slt, eq.astype(BF16)) + carry
            tie_ok = jnp.where(kj == thr, jnp.where(pref < need, 1.0, 0.0), 0.0)
            chosen = jnp.where(kj > thr, 1.0, tie_ok)
            chosen = jnp.where(causal(j), chosen, 0.0)
            sel_ref[j] = jnp.where(chosen > 0.0, 0.0, NEG_INF)
            return carry + jnp.sum(eq, axis=0, keepdims=True)

        lax.fori_loop(0, nb, sel_blk, jnp.zeros((1, tq), F32))

    @pl.when(jnp.logical_not(excess))
    def _():
        def sel_blk(j, c):
            sel_ref[j] = jnp.where(key_ref[j] >= thr, jnp.where(causal(j), 0.0, NEG_INF), NEG_INF)
            return c

        lax.fori_loop(0, nb, sel_blk, 0)

    def logits(j, p):
        bias = jnp.concatenate([bias_ref[jnp.clip(sub * i + u - j, 0, 2), p] for u in range(sub)], axis=1)
        sel = sel_ref[j]
        s = _dot_nt(k2_ref[j], zq_ref[:, p * LANES:(p + 1) * LANES])
        return s + bias + jnp.concatenate([sel, sel], axis=0)

    def col_max(x):
        return jnp.max(jnp.max(x.reshape(kb // 8, 8, tq), axis=0), axis=0, keepdims=True)

    acc_ref[...] = jnp.zeros(acc_ref.shape, F32)
    quarter = kb // 2

    def att_grp(g, ms):
        out = []
        for p in range(npair):
            ss = [logits(sub * g + u, p) for u in range(sub)]
            pes = [[None, None] for _ in range(sub)]
            als, mn = [], []
            for hd in range(2):
                sl = [x[hd * kb:(hd + 1) * kb] for x in ss]
                m_old = ms[2 * p + hd]
                m_new = m_old
                for x in sl:
                    m_new = jnp.maximum(m_new, col_max(x))
                m_safe = jnp.where(m_new == NEG_INF, 0.0, m_new)
                for u in range(sub):
                    pes[u][hd] = jnp.exp(sl[u] - m_safe)
                als.append(jnp.broadcast_to(jnp.exp(m_old - m_safe), (quarter, tq)))
                mn.append(m_new)
            upd = acc_ref[p] * jnp.concatenate(als + als, axis=0)
            for u in range(sub):
                upd = upd + _dot(v2t_ref[sub * g + u], jnp.concatenate(pes[u], axis=0).astype(BF16))
            acc_ref[p] = upd
            out += mn
        return tuple(out)

    lax.fori_loop(0, i + 1, att_grp, tuple(jnp.full((1, tq), NEG_INF, F32) for _ in range(ATT_HEADS)))
    for p in range(npair):
        a = acc_ref[p]
        o_ref[:, p * LANES:(p + 1) * LANES] = (a[:kb] / a[kb:]).T.astype(o_ref.dtype)


def _t5_bucket(rel):
    max_exact = REL_BUCKETS // 2
    rel_f = jnp.maximum(rel, 1).astype(F32)
    large = max_exact + (jnp.log(rel_f / max_exact) / math.log(REL_MAX_DIST / max_exact)
                         * (REL_BUCKETS - max_exact)).astype(I32)
    large = jnp.minimum(large, REL_BUCKETS - 1)
    return jnp.where(rel < max_exact, rel, large)


def _bias_tiles(rel_bias):
    kb = KEY_BLOCK
    assert kb >= REL_MAX_DIST
    tq = jnp.arange(kb)[:, None]
    sk = jnp.arange(kb)[None, :]
    tiles = []
    for off in range(3):
        rel = jnp.maximum(off * kb + tq - sk, 0)
        bucket = _t5_bucket(rel)[..., None]
        tile = jnp.zeros((kb, kb, rel_bias.shape[1]), F32)
        for b in range(REL_BUCKETS):
            tile = jnp.where(bucket == b, rel_bias[b].astype(F32), tile)
        per_head = jnp.transpose(tile, (2, 1, 0))
        tiles.append(jnp.concatenate([per_head[0::2], per_head[1::2]], axis=-2))
    return jnp.stack(tiles, axis=0).astype(F32)


def _dsa(zq, zs, bias_tiles, bsz, seq_len):
    t = zq.shape[0]
    nq = seq_len // Q_ROWS
    nkb = seq_len // KEY_BLOCK
    k_sel = min(TOPK_MAX, seq_len // 4)
    aw = ATT_HEADS * ATT_DH
    npair = ATT_HEADS // 2
    return pl.pallas_call(
        functools.partial(_dsa_body, k_sel=k_sel),
        grid=(bsz, nq),
        in_specs=[
            pl.BlockSpec((Q_ROWS, zq.shape[1]), lambda b, i: (b * nq + i, 0)),
            pl.BlockSpec((Q_ROWS, zs.shape[1]), lambda b, i: (b * nq + i, 0)),
            pl.BlockSpec((seq_len, zs.shape[1]), lambda b, i: (b, 0)),
            pl.BlockSpec(bias_tiles.shape, lambda b, i: (0, 0, 0, 0)),
        ],
        out_specs=pl.BlockSpec((Q_ROWS, aw), lambda b, i: (b * nq + i, 0)),
        out_shape=jax.ShapeDtypeStruct((t, aw), BF16),
        scratch_shapes=[
            pltpu.VMEM((nkb, KEY_BLOCK, Q_ROWS), I32),
            pltpu.VMEM((nkb, KEY_BLOCK, Q_ROWS), I16),
            pltpu.VMEM((nkb, KEY_BLOCK, Q_ROWS), I16),
            pltpu.VMEM((nkb, KEY_BLOCK, Q_ROWS), F32),
            pltpu.VMEM((nkb, 2 * KEY_BLOCK, LANES), BF16),
            pltpu.VMEM((nkb, 2 * KEY_BLOCK, LANES), BF16),
            pltpu.VMEM((nkb, 2 * LANES, 2 * KEY_BLOCK), BF16),
            pltpu.VMEM((npair, 2 * LANES, Q_ROWS), F32),
        ],
        compiler_params=_params(("arbitrary", "arbitrary")),
        name="dsa",
    )(zq, zs, zs, bias_tiles)


def _merge_body(x_ref, oa_ref, ob_ref, zg_ref, gm_ref, shf_ref, scf_ref, nw_ref,
                wpa, wpb, wout, wrt, br, x1_ref, h2_ref, lgt_ref):
    d = x_ref.shape[1]
    ya = _dot(oa_ref[...], wpa[...])
    yb = _dot(ob_ref[...], wpb[...])
    merged = _sigmoid(zg_ref[:, :d]) * ya + _sigmoid(zg_ref[:, d:]) * yb
    y = _dot(merged.astype(BF16), wout[...])
    x1 = x_ref[...] + gm_ref[0] * y
    x1_ref[...] = x1
    h2 = _rms_mod(x1, nw_ref[...], scf_ref[0], shf_ref[0])
    for j in range(TOK_TILE):
        h2_ref[pl.ds(j, x_ref.shape[0], stride=TOK_TILE), :] = h2[:, j * LANES:(j + 1) * LANES]
    hh, hl = _split2(h2)
    wh, wl = _split2(wrt[...])
    lgt_ref[...] = _dot_nt(wh, hh) + (_dot_nt(wh, hl) + _dot_nt(wl, hh)) + br[...]


def _merge(x2d, oa, ob, zg, gate_m, shift_f, scale_f, nw, wpa, wpb, wout, wrt, br, seq_len):
    t, d = x2d.shape
    assert d == TOK_TILE * LANES
    tm = 256
    per_b = seq_len // tm
    e = wrt.shape[0]
    row = lambda i: (i, 0)
    const = lambda i: (0, 0)
    bmap = lambda i: (i // per_b, 0, 0)
    return pl.pallas_call(
        _merge_body,
        grid=(t // tm,),
        in_specs=[
            pl.BlockSpec((tm, d), row),
            pl.BlockSpec((tm, oa.shape[1]), row),
            pl.BlockSpec((tm, ob.shape[1]), row),
            pl.BlockSpec((tm, zg.shape[1]), row),
            pl.BlockSpec((1, 1, d), bmap),
            pl.BlockSpec((1, 1, d), bmap),
            pl.BlockSpec((1, 1, d), bmap),
            pl.BlockSpec((1, d), const),
            pl.BlockSpec(wpa.shape, const),
            pl.BlockSpec(wpb.shape, const),
            pl.BlockSpec(wout.shape, const),
            pl.BlockSpec(wrt.shape, const),
            pl.BlockSpec(br.shape, const),
        ],
        out_specs=[pl.BlockSpec((tm, d), row), pl.BlockSpec((tm * TOK_TILE, LANES), row),
                   pl.BlockSpec((e, tm), lambda i: (0, i))],
        out_shape=[jax.ShapeDtypeStruct((t, d), F32), jax.ShapeDtypeStruct((t * TOK_TILE, LANES), F32),
                   jax.ShapeDtypeStruct((e, t), F32)],
        compiler_params=_params(("parallel",)),
        name="merge",
    )(x2d, oa, ob, zg, gate_m, shift_f, scale_f, nw, wpa, wpb, wout, wrt, br)


def _route1_body(lg_ref, er_ref, gt_ref, cnt_ref, base_ref):
    i = pl.program_id(0)
    e, tr = lg_ref.shape

    @pl.when(i == 0)
    def _():
        base_ref[...] = jnp.zeros_like(base_ref)

    l = lg_ref[...]
    eidx = lax.broadcasted_iota(I32, (e, tr), 0).astype(F32)
    vals, es, ohs = [], [], []
    for _ in range(TOP_K):
        m = jnp.max(l, axis=0, keepdims=True)
        ek = jnp.min(jnp.where(l == m, eidx, float(e)), axis=0, keepdims=True)
        oh = eidx == ek
        l = jnp.where(oh, NEG_INF, l)
        vals.append(m)
        es.append(ek)
        ohs.append(oh)
    ps = [jnp.exp(v - vals[0]) for v in vals]
    den = ps[0] + ps[1] + ps[2] + ps[3]
    gs = [p / den for p in ps]

    member = jnp.zeros((e, tr), F32)
    for oh in ohs:
        member = member + jnp.where(oh, 1.0, 0.0)
    mb = member.astype(BF16)
    r2 = lax.broadcasted_iota(I32, (tr, tr), 0)
    c2 = lax.broadcasted_iota(I32, (tr, tr), 1)
    sut = jnp.where(r2 < c2, 1.0, 0.0).astype(BF16)
    base = base_ref[...]
    pref = _dot(mb, sut) + jnp.concatenate([base] * (tr // LANES), axis=1)
    ranks = [jnp.sum(jnp.where(oh, pref, 0.0), axis=0, keepdims=True) for oh in ohs]
    er_ref[...] = jnp.concatenate(es + ranks, axis=0).astype(I32)

    row8 = lax.broadcasted_iota(I32, (8, tr), 0)
    g8 = jnp.zeros((8, tr), F32)
    for kk in range(TOP_K):
        g8 = jnp.where(row8 == kk, gs[kk], g8)
    gfull = jnp.concatenate([g8, jnp.zeros((LANES - 8, tr), F32)], axis=0)
    gt_ref[...] = gfull.T

    new_base = base + _dot(mb, jnp.ones((tr, LANES), BF16))
    base_ref[...] = new_base
    cnt_ref[...] = new_base


def _route1(lgt):
    e, t = lgt.shape
    tr = 512
    return pl.pallas_call(
        _route1_body,
        grid=(t // tr,),
        in_specs=[pl.BlockSpec((e, tr), lambda i: (0, i))],
        out_specs=[pl.BlockSpec((2 * TOP_K, tr), lambda i: (0, i)),
                   pl.BlockSpec((tr, LANES), lambda i: (i, 0)),
                   pl.BlockSpec((e, LANES), lambda i: (0, 0))],
        out_shape=[jax.ShapeDtypeStruct((2 * TOP_K, t), I32),
                   jax.ShapeDtypeStruct((t, LANES), F32),
                   jax.ShapeDtypeStruct((e, LANES), F32)],
        scratch_shapes=[pltpu.VMEM((e, LANES), F32)],
        compiler_params=_params(("arbitrary",)),
        name="route_topk",
    )(lgt)


def _route2_body(er_ref, cnt_ref, slot_ref, be_ref, tot_ref, fs_ref):
    e = cnt_ref.shape[0]
    tt = er_ref.shape[1]
    nbp = be_ref.shape[1]
    sh = int(math.log2(MOE_BLOCK))
    cnt = cnt_ref[...].astype(I32)
    padded = ((cnt + (MOE_BLOCK - 1)) >> sh) << sh
    rows = []
    acc = jnp.zeros((1, LANES), I32)
    for ei in range(e):
        rows.append(acc)
        acc = acc + padded[ei:ei + 1, :]
    pstart = jnp.concatenate(rows, axis=0)
    pend = pstart + padded
    tot_ref[...] = jnp.broadcast_to(acc, tot_ref.shape)
    fs_ref[...] = jnp.where(padded > 0, pend - MOE_BLOCK, -1)

    er = er_ref[...]
    eidx = lax.broadcasted_iota(I32, (e, tt), 0)
    pst = jnp.concatenate([pstart] * (tt // LANES), axis=1)
    outs = []
    for kk in range(TOP_K):
        ps = jnp.sum(jnp.where(eidx == er[kk:kk + 1, :], pst, 0), axis=0, keepdims=True)
        outs.append(ps + er[TOP_K + kk:TOP_K + kk + 1, :])
    slot_ref[...] = jnp.concatenate(outs + [jnp.zeros((TOP_K, tt), I32)], axis=0)

    blk0 = lax.broadcasted_iota(I32, (e, nbp), 1) * MOE_BLOCK
    pe = jnp.concatenate([pend] * (nbp // LANES), axis=1)
    nle = jnp.sum(jnp.where(pe <= blk0, 1, 0), axis=0, keepdims=True)
    be_ref[...] = jnp.broadcast_to(jnp.minimum(nle, e - 1), be_ref.shape)


def _route2(er, cnt, nbp):
    t = er.shape[1]
    tt = min(2048, t)
    e = cnt.shape[0]
    return pl.pallas_call(
        _route2_body,
        grid=(t // tt,),
        in_specs=[pl.BlockSpec((2 * TOP_K, tt), lambda i: (0, i)),
                  pl.BlockSpec((e, LANES), lambda i: (0, 0))],
        out_specs=[pl.BlockSpec((2 * TOP_K, tt), lambda i: (0, i)),
                   pl.BlockSpec((8, nbp), lambda i: (0, 0)),
                   pl.BlockSpec((8, LANES), lambda i: (0, 0)),
                   pl.BlockSpec((e, LANES), lambda i: (0, 0))],
        out_shape=[jax.ShapeDtypeStruct((2 * TOP_K, t), I32),
                   jax.ShapeDtypeStruct((8, nbp), I32),
                   jax.ShapeDtypeStruct((8, LANES), I32),
                   jax.ShapeDtypeStruct((e, LANES), I32)],
        compiler_params=_params(("arbitrary",)),
        name="route_slots",
    )(er, cnt)


def _tile(row):
    return pl.ds(pl.multiple_of(row * TOK_TILE, TOK_TILE), TOK_TILE)


def _dispatch_body(fs_ref, nu_ref, slot_ref, h_ref, xs_ref, zbuf, sem, zsem):
    i = pl.program_id(0)
    td = slot_ref.shape[1]

    @pl.when(i == 0)
    def _():
        zbuf[...] = jnp.zeros_like(zbuf)
        zrows = zbuf.shape[0]
        nblk = xs_ref.shape[0] // zrows
        fills = []
        for e in range(fs_ref.shape[0]):
            fills.append((fs_ref[e] >= 0, jnp.maximum(fs_ref[e], 0) * TOK_TILE))
        for b in range(fs_ref.shape[0]):
            blk = nu_ref[0] + b
            fills.append((blk < nblk, jnp.minimum(blk, nblk - 1) * zrows))
        for pred, start in fills:
            @pl.when(pred)
            def _(start=start):
                pltpu.make_async_copy(zbuf, xs_ref.at[pl.ds(pl.multiple_of(start, zrows), zrows)], zsem).start()
        for pred, _ in fills:
            @pl.when(pred)
            def _():
                pltpu.make_async_copy(zbuf, xs_ref.at[pl.ds(0, zrows)], zsem).wait()

    def issue(g, c):
        for u in range(DMA_UNROLL):
            tok = g * DMA_UNROLL + u
            for kk in range(TOP_K):
                pltpu.make_async_copy(h_ref.at[_tile(tok)], xs_ref.at[_tile(slot_ref[kk, tok])], sem).start(
                    priority=(u * TOP_K + kk) % 2)
        return c

    lax.fori_loop(0, td // DMA_UNROLL, issue, 0)
    for kk in range(TOP_K):
        pltpu.make_async_copy(h_ref, xs_ref.at[pl.ds(0, td * TOK_TILE)], sem).wait()


def _dispatch(fill_start, nused, slot, h2, n_slots):
    t = h2.shape[0] // TOK_TILE
    td = 512
    grid_spec = pltpu.PrefetchScalarGridSpec(
        num_scalar_prefetch=2,
        grid=(t // td,),
        in_specs=[pl.BlockSpec((2 * TOP_K, td), lambda i, fs, nu: (0, i), memory_space=pltpu.SMEM),
                  pl.BlockSpec((td * TOK_TILE, LANES), lambda i, fs, nu: (i, 0))],
        out_specs=pl.BlockSpec(memory_space=pl.ANY),
        scratch_shapes=[pltpu.VMEM((MOE_BLOCK * TOK_TILE, LANES), h2.dtype), pltpu.SemaphoreType.DMA, pltpu.SemaphoreType.DMA],
    )
    return pl.pallas_call(
        _dispatch_body,
        grid_spec=grid_spec,
        out_shape=jax.ShapeDtypeStruct((n_slots * TOK_TILE, LANES), h2.dtype),
        compiler_params=_params(("arbitrary",)),
        name="moe_dispatch",
    )(fill_start, nused, slot, h2)


def _expert_body(be_ref, nu_ref, xs_ref, w1_ref, b1_ref, w2_ref, b2_ref, y_ref, w1b, w2b):
    i = pl.program_id(0)
    f = w2_ref.shape[2]

    @pl.when(jnp.logical_or(i == 0, be_ref[i] != be_ref[jnp.maximum(i - 1, 0)]))
    def _():
        w1b[...] = w1_ref[0, 0].astype(BF16)
        w2b[...] = w2_ref[0, 0].astype(BF16)

    @pl.when(i < nu_ref[0])
    def _():
        x = jnp.concatenate([xs_ref[pl.ds(j, MOE_BLOCK, stride=TOK_TILE), :] for j in range(TOK_TILE)], axis=1)
        hm = _dot(x.astype(BF16), w1b[...]) + b1_ref[0, 0]
        glu = jnp.minimum(hm[:, :f], SWIGLU_LIMIT)
        lin = jnp.clip(hm[:, f:], -SWIGLU_LIMIT, SWIGLU_LIMIT)
        act = glu * _sigmoid(SWIGLU_ALPHA * glu) * (lin + 1.0)
        y = _dot(act.astype(BF16), w2b[...]) + b2_ref[0, 0]
        for j in range(TOK_TILE):
            y_ref[pl.ds(j, MOE_BLOCK, stride=TOK_TILE), :] = y[:, j * LANES:(j + 1) * LANES]

    @pl.when(i >= nu_ref[0])
    def _():
        y_ref[...] = jnp.zeros_like(y_ref)


def _experts(be, nused, xs, w1, b1, w2, b2, layer):
    ns = xs.shape[0] // TOK_TILE
    depth, e, d, f2 = w1.shape
    f = w2.shape[2]
    nblk = ns // MOE_BLOCK
    used = lambda i, nu: jnp.minimum(i, nu[0] - 1)
    grid_spec = pltpu.PrefetchScalarGridSpec(
        num_scalar_prefetch=2,
        grid=(nblk,),
        in_specs=[
            pl.BlockSpec((MOE_BLOCK * TOK_TILE, LANES), lambda i, be, nu: (used(i, nu), 0)),
            pl.BlockSpec((1, 1, d, f2), lambda i, be, nu: (layer, be[i], 0, 0)),
            pl.BlockSpec((1, 1, 1, f2), lambda i, be, nu: (layer, be[i], 0, 0)),
            pl.BlockSpec((1, 1, f, d), lambda i, be, nu: (layer, be[i], 0, 0)),
            pl.BlockSpec((1, 1, 1, d), lambda i, be, nu: (layer, be[i], 0, 0)),
        ],
        out_specs=pl.BlockSpec((MOE_BLOCK * TOK_TILE, LANES), lambda i, be, nu: (i, 0)),
        scratch_shapes=[pltpu.VMEM((d, f2), BF16), pltpu.VMEM((f, d), BF16)],
    )
    return pl.pallas_call(
        _expert_body,
        grid_spec=grid_spec,
        out_shape=jax.ShapeDtypeStruct((ns * TOK_TILE, LANES), F32),
        compiler_params=_params(("arbitrary",)),
        name="moe_experts",
    )(be, nused, xs, w1, b1.reshape(depth, e, 1, f2), w2, b2.reshape(depth, e, 1, d))


def _combine_body(slot_ref, slotn_ref, y_ref, x1_ref, gt_ref, gf_ref, fw_ref, o_ref, buf0, buf1, sem0, sem1, *, final):
    i = pl.program_id(0)
    n = pl.num_programs(0)
    td = x1_ref.shape[0]

    def issue_all(sref, buf, sem):
        def issue(g, c):
            for u in range(DMA_UNROLL):
                tok = g * DMA_UNROLL + u
                for kk in range(TOP_K):
                    pltpu.make_async_copy(y_ref.at[_tile(sref[kk, tok])], buf.at[_tile(kk * td + tok)], sem).start(
                        priority=(u * TOP_K + kk) % 2)
            return c

        lax.fori_loop(0, td // DMA_UNROLL, issue, 0)

    def finish(buf, sem):
        for kk in range(TOP_K):
            pltpu.make_async_copy(y_ref.at[pl.ds(0, td * TOK_TILE)], buf.at[pl.ds(kk * td * TOK_TILE, td * TOK_TILE)], sem).wait()
        g = gt_ref[...]
        gk = [g[:, kk:kk + 1] for kk in range(TOP_K)]
        parts = []
        for j in range(TOK_TILE):
            y = gk[0] * buf[pl.ds(j, td, stride=TOK_TILE), :]
            for kk in range(1, TOP_K):
                y = y + gk[kk] * buf[pl.ds(kk * td * TOK_TILE + j, td, stride=TOK_TILE), :]
            parts.append(x1_ref[:, j * LANES:(j + 1) * LANES] + gf_ref[0][:, j * LANES:(j + 1) * LANES] * y)
        x2 = jnp.concatenate(parts, axis=-1)
        if final:
            ms = jnp.mean(x2 * x2, axis=-1, keepdims=True)
            x2 = x2 * lax.rsqrt(ms + NORM_EPS) * fw_ref[...]
        o_ref[...] = x2

    even = (i % 2) == 0
    more = i + 1 < n

    @pl.when(i == 0)
    def _():
        issue_all(slot_ref, buf0, sem0)

    @pl.when(jnp.logical_and(more, even))
    def _():
        issue_all(slotn_ref, buf1, sem1)

    @pl.when(jnp.logical_and(more, jnp.logical_not(even)))
    def _():
        issue_all(slotn_ref, buf0, sem0)

    @pl.when(even)
    def _():
        finish(buf0, sem0)

    @pl.when(jnp.logical_not(even))
    def _():
        finish(buf1, sem1)


def _combine(slot, ys, x1, gt, gate_f, fw, seq_len, final):
    t, d = x1.shape
    td = 256
    per_b = seq_len // td
    nstep = t // td
    rows = TOP_K * td * TOK_TILE
    return pl.pallas_call(
        functools.partial(_combine_body, final=final),
        grid=(nstep,),
        in_specs=[pl.BlockSpec((2 * TOP_K, td), lambda i: (0, i), memory_space=pltpu.SMEM),
                  pl.BlockSpec((2 * TOP_K, td), lambda i: (0, jnp.minimum(i + 1, nstep - 1)), memory_space=pltpu.SMEM),
                  pl.BlockSpec(memory_space=pl.ANY),
                  pl.BlockSpec((td, d), lambda i: (i, 0)),
                  pl.BlockSpec((td, LANES), lambda i: (i, 0)),
                  pl.BlockSpec((1, 1, d), lambda i: (i // per_b, 0, 0)),
                  pl.BlockSpec((1, d), lambda i: (0, 0))],
        out_specs=pl.BlockSpec((td, d), lambda i: (i, 0)),
        out_shape=jax.ShapeDtypeStruct((t, d), F32),
        scratch_shapes=[pltpu.VMEM((rows, LANES), F32), pltpu.VMEM((rows, LANES), F32),
                        pltpu.SemaphoreType.DMA, pltpu.SemaphoreType.DMA],
        compiler_params=_params(("arbitrary",)),
        name="moe_combine",
    )(slot, slot, ys, x1, gt, gate_f, fw)


def _split_w_in(w):
    hw = HGRN_HEADS * HGRN_DK
    aw = ATT_HEADS * ATT_DH
    iw = IDX_HEADS * IDX_DH
    d = w.shape[0]
    o = 4 * hw
    qb = w[:, o:o + aw]
    kb = w[:, o + aw:o + aw + ATT_DH]
    vb = w[:, o + aw + ATT_DH:o + aw + 2 * ATT_DH]
    o2 = o + aw + 2 * ATT_DH
    iq = w[:, o2:o2 + iw]
    ik = w[:, o2 + iw:o2 + iw + IDX_DH]
    ih = w[:, o2 + iw + IDX_DH:o2 + iw + IDX_DH + IDX_HEADS]
    o3 = o2 + iw + IDX_DH + IDX_HEADS
    small = jnp.concatenate([kb, vb, ik, ih, jnp.zeros((d, 2 * LANES - 3 * ATT_DH - IDX_HEADS), w.dtype)], axis=1)
    groups = (w[:, :o], jnp.concatenate([qb, iq], axis=1), small, w[:, o3:])
    return tuple(g.astype(BF16) for g in groups)


def kernel(x, c, w_ada, b_ada, norm_mix_w, w_in, hgrn_lb_logits, hgrn_norm_w, rel_bias, w_proj_a, w_proj_b,
           w_out, norm_ffn_w, w_router, b_router, w_mlp1, b_mlp1, w_mlp2, b_mlp2, final_norm_w):
    bsz, seq_len, d = x.shape
    depth = w_in.shape[0]
    n_exp = w_router.shape[2]
    t = bsz * seq_len
    n_slots = t * TOP_K + n_exp * MOE_BLOCK
    nblk = n_slots // MOE_BLOCK
    nbp = -(-nblk // LANES) * LANES

    mod = _ada(c, w_ada, b_ada)
    bias_tiles = _bias_tiles(rel_bias)
    x2d = x.reshape(t, d)
    for l in range(depth):
        m6 = mod[l].reshape(bsz, N_MOD, 1, d)
        shift_m, scale_m, gate_m, shift_f, scale_f, gate_f = (m6[:, n] for n in range(N_MOD))
        zh, zq, zs, zg = _inproj(x2d, shift_m, scale_m, norm_mix_w[l].reshape(1, d), _split_w_in(w_in[l]), seq_len)
        oa = _hgrn(zh, hgrn_lb_logits, hgrn_norm_w[l], l, bsz, seq_len).reshape(t, -1)
        ob = _dsa(zq, zs, bias_tiles, bsz, seq_len)
        x1, h2, lgt = _merge(x2d, oa, ob, zg, gate_m, shift_f, scale_f, norm_ffn_w[l].reshape(1, d),
                             w_proj_a[l].astype(BF16), w_proj_b[l].astype(BF16), w_out[l].astype(BF16),
                             w_router[l].T, b_router[l].reshape(n_exp, 1), seq_len)
        er, gt, cnt = _route1(lgt)
        slot, be, tot, fill_start = _route2(er, cnt, nbp)
        nused = (tot[0, :1] >> int(math.log2(MOE_BLOCK))).astype(I32)
        xs = _dispatch(fill_start[:, 0], nused, slot, h2, n_slots)
        ys = _experts(be[0, :nblk], nused, xs, w_mlp1, b_mlp1, w_mlp2, b_mlp2, l)
        x2d = _combine(slot, ys, x1, gt, gate_f, final_norm_w.reshape(1, d), seq_len, final=(l == depth - 1))
    return x2d.reshape(bsz, seq_len, d)
```
